```python
import jax, jax.numpy as jnp
from jax import lax
import numpy as np

D_MODEL = 1024
BATCH = 4
SEQ = 8192
DEPTH = 2
DEC_BATCH = 1
DEC_SEQ = 16384
PAST_LEN = 128

GRID_W = 64
N_MEM = 256
MIX_WIDTH = 3 * D_MODEL // 2
XATTN_WIDTH = D_MODEL // 2
BRANCH_WIDTH = MIX_WIDTH + XATTN_WIDTH
IN_WIDTH = 3 * MIX_WIDTH + XATTN_WIDTH + BRANCH_WIDTH
NA_HEAD_DIM = 64
NA_HEADS = MIX_WIDTH // NA_HEAD_DIM
XATTN_HEADS = 4
XATTN_HEAD_DIM = XATTN_WIDTH // XATTN_HEADS
CONV_WIDTH = 3
NA_WIN_H = 8
NA_WIN_W = 16
NA_QBLOCK_W = 16
NA_KBLOCK_W = NA_QBLOCK_W + NA_WIN_W
N_CONV_LAYERS = (DEPTH + 1) // 2
N_NA_LAYERS = DEPTH // 2
RMS_EPS = 1e-6
NEG_INF = -1e30

kernel_name = 'hybrid_shortconv_natten_memory_encoder'


def rms_norm(x, w):
    x32 = x.astype(jnp.float32)
    y = x32 * lax.rsqrt(jnp.mean(x32 * x32, axis=-1, keepdims=True) + RMS_EPS)
    return (y * w.astype(jnp.float32)).astype(x.dtype)


def short_conv_mixer(b_gate, c_gate, u, conv_w, conv_b):
    seq = u.shape[1]
    half = CONV_WIDTH // 2
    v = jnp.pad(c_gate * u, ((0, 0), (half, half), (0, 0)))
    conv = sum(v[:, j:j + seq] * conv_w[j] for j in range(CONV_WIDTH)) + conv_b
    return b_gate * conv


def _na_column_tables():
    n_cb = GRID_W // NA_QBLOCK_W
    qcol = np.arange(GRID_W).reshape(n_cb, NA_QBLOCK_W)
    kstart = np.clip(np.arange(n_cb) * NA_QBLOCK_W - NA_WIN_W // 2, 0, GRID_W - NA_KBLOCK_W)
    kcol = kstart[:, None] + np.arange(NA_KBLOCK_W)[None, :]
    cstart = np.clip(qcol - NA_WIN_W // 2, 0, GRID_W - NA_WIN_W)
    rel = kcol[:, None, :] - cstart[:, :, None]
    valid = (rel >= 0) & (rel < NA_WIN_W)
    dx = np.clip(kcol[:, None, :] - qcol[:, :, None] + NA_WIN_W - 1, 0, 2 * NA_WIN_W - 2)
    return kcol.astype(np.int32), valid, dx.astype(np.int32)


def neighborhood_attention(q, k, v, rpb):
    bsz, seq = q.shape[0], q.shape[1]
    rows = seq // GRID_W
    win_h = min(NA_WIN_H, rows)
    n_cb = GRID_W // NA_QBLOCK_W
    kcol_np, valid_np, dx_np = _na_column_tables()
    kcol = jnp.asarray(kcol_np)
    mask = jnp.asarray(valid_np)[:, :, None, :]
    dx = jnp.asarray(dx_np)
    scale = NA_HEAD_DIM ** -0.5
    kg = k.reshape(bsz, rows, GRID_W, NA_HEADS, NA_HEAD_DIM)
    vg = v.reshape(bsz, rows, GRID_W, NA_HEADS, NA_HEAD_DIM)
    q_rows = jnp.moveaxis(q.reshape(bsz, rows, GRID_W, NA_HEADS, NA_HEAD_DIM), 1, 0)
    q_rows = q_rows.reshape(rows, bsz, n_cb, NA_QBLOCK_W, NA_HEADS, NA_HEAD_DIM)

    def one_row(args):
        r, q_r = args
        r0 = jnp.clip(r - win_h // 2, 0, rows - win_h)
        k_r = lax.dynamic_slice_in_dim(kg, r0, win_h, axis=1)
        v_r = lax.dynamic_slice_in_dim(vg, r0, win_h, axis=1)
        k_b = k_r[:, :, kcol]
        v_b = v_r[:, :, kcol]
        s = jnp.einsum('bnihd,banjhd->bhniaj', q_r, k_b).astype(jnp.float32) * scale
        dy = r0 + jnp.arange(win_h) - r + NA_WIN_H - 1
        bias = rpb[:, dy[:, None, None, None], dx[None]]
        s = s + jnp.transpose(bias, (0, 2, 3, 1, 4)).astype(jnp.float32)
        s = jnp.where(mask, s, NEG_INF)
        p = jax.nn.softmax(s.reshape(s.shape[:4] + (-1,)), axis=-1).reshape(s.shape)
        o = jnp.einsum('bhniaj,banjhd->bnihd', p.astype(v_b.dtype), v_b)
        return o.reshape(bsz, GRID_W, NA_HEADS, NA_HEAD_DIM)

    out = lax.map(one_row, (jnp.arange(rows), q_rows))
    return jnp.moveaxis(out, 0, 1).reshape(bsz, seq, NA_HEADS * NA_HEAD_DIM)


def memory_attention(q, mem_k, mem_v):
    bsz, seq = q.shape[0], q.shape[1]
    s = jnp.einsum('bshd,bmhd->bhsm', q, mem_k).astype(jnp.float32) * (XATTN_HEAD_DIM ** -0.5)
    p = jax.nn.softmax(s, axis=-1)
    o = jnp.einsum('bhsm,bmhd->bshd', p.astype(mem_v.dtype), mem_v)
    return o.reshape(bsz, seq, XATTN_WIDTH)


def _trunk(x, mem, norm_w, w_in, w_out, mem_norm_w, w_mem_kv, conv_w, conv_b, na_rpb, final_norm_w):
    bsz, seq, _ = x.shape
    splits = [MIX_WIDTH, 2 * MIX_WIDTH, 3 * MIX_WIDTH, 3 * MIX_WIDTH + XATTN_WIDTH]
    for i in range(DEPTH):
        h = rms_norm(x, norm_w[i])
        z = h @ w_in[i]
        p0, p1, p2, q_mem, gate = jnp.split(z, splits, axis=-1)
        if i % 2 == 0:
            mix = short_conv_mixer(p0, p1, p2, conv_w[i // 2], conv_b[i // 2])
        else:
            hs = (bsz, seq, NA_HEADS, NA_HEAD_DIM)
            mix = neighborhood_attention(p0.reshape(hs), p1.reshape(hs), p2.reshape(hs), na_rpb[i // 2])
        mkv = rms_norm(mem, mem_norm_w[i]) @ w_mem_kv[i]
        mk, mv = jnp.split(mkv, 2, axis=-1)
        ms = (mem.shape[0], mem.shape[1], XATTN_HEADS, XATTN_HEAD_DIM)
        xo = memory_attention(q_mem.reshape(bsz, seq, XATTN_HEADS, XATTN_HEAD_DIM), mk.reshape(ms), mv.reshape(ms))
        y = jnp.concatenate([mix, xo], axis=-1) * jax.nn.silu(gate)
        x = x + y @ w_out[i]
    return rms_norm(x, final_norm_w)


def setup_inputs(seed: int = 0) -> dict:
    key = jax.random.key(seed)
    ks = jax.random.split(key, 13)
    f32 = jnp.float32

    def nrm(k, shape, s):
        return jax.random.normal(k, shape, f32) * s

    return {
        'x_prompt': nrm(ks[0], (BATCH, SEQ, D_MODEL), 1.0),
        'x_sample': nrm(ks[1], (DEC_BATCH, DEC_SEQ, D_MODEL), 1.0),
        'mem_prompt': nrm(ks[2], (BATCH, N_MEM, D_MODEL), 1.0),
        'mem_sample': nrm(ks[3], (DEC_BATCH, N_MEM, D_MODEL), 1.0),
        'norm_w': 1.0 + nrm(ks[4], (DEPTH, D_MODEL), 0.02),
        'w_in': nrm(ks[5], (DEPTH, D_MODEL, IN_WIDTH), D_MODEL ** -0.5),
        'w_out': nrm(ks[6], (DEPTH, BRANCH_WIDTH, D_MODEL), BRANCH_WIDTH ** -0.5),
        'mem_norm_w': 1.0 + nrm(ks[7], (DEPTH, D_MODEL), 0.02),
        'w_mem_kv': nrm(ks[8], (DEPTH, D_MODEL, 2 * XATTN_WIDTH), D_MODEL ** -0.5),
        'conv_w': nrm(ks[9], (N_CONV_LAYERS, CONV_WIDTH, MIX_WIDTH), CONV_WIDTH ** -0.5),
        'conv_b': nrm(ks[10], (N_CONV_LAYERS, MIX_WIDTH), 0.02),
        'na_rpb': nrm(ks[11], (N_NA_LAYERS, NA_HEADS, 2 * NA_WIN_H - 1, 2 * NA_WIN_W - 1), 0.1),
        'final_norm_w': 1.0 + nrm(ks[12], (D_MODEL,), 0.02),
    }


def reference(x_prompt, x_sample, mem_prompt, mem_sample, norm_w, w_in, w_out, mem_norm_w, w_mem_kv,
              conv_w, conv_b, na_rpb, final_norm_w):
    y_prompt = _trunk(x_prompt, mem_prompt, norm_w, w_in, w_out, mem_norm_w, w_mem_kv,
                      conv_w, conv_b, na_rpb, final_norm_w)
    y_sample = _trunk(x_sample, mem_sample, norm_w, w_in, w_out, mem_norm_w, w_mem_kv,
                      conv_w, conv_b, na_rpb, final_norm_w)
    return (y_prompt, y_sample)
```

```python
import functools

import numpy as np
import jax
import jax.numpy as jnp
from jax import lax
from jax.experimental import pallas as pl
from jax.experimental.pallas import tpu as pltpu

D_MODEL = 1024
GRID_W = 64
N_MEM = 256
MIX_WIDTH = 1536
XATTN_WIDTH = 512
BRANCH_WIDTH = MIX_WIDTH + XATTN_WIDTH
NA_HEAD_DIM = 64
NA_HEADS = MIX_WIDTH // NA_HEAD_DIM
NA_HEAD_PAIRS = NA_HEADS // 2
XATTN_HEADS = 4
XATTN_HEAD_DIM = XATTN_WIDTH // XATTN_HEADS
NA_WIN_H = 8
NA_WIN_W = 16
RMS_EPS = 1e-6
NEG_INF = -1e30

OFF_P1 = MIX_WIDTH
OFF_P2 = 2 * MIX_WIDTH
OFF_QMEM = 3 * MIX_WIDTH
OFF_GATE = 3 * MIX_WIDTH + XATTN_WIDTH

LANES = 128
BF16_SUBLANES = 16
TOKEN_TILE = 512
CHUNK = 512
NA_TILE_ROWS = 8
NA_TILE = NA_TILE_ROWS * GRID_W
NA_HALO = 4 * GRID_W
NA_KEYS = NA_WIN_H * GRID_W
VMEM_LIMIT = 56 * 1024 * 1024

_BF16 = jnp.bfloat16
_F32 = jnp.float32


def _rms(x, w):
    return x * lax.rsqrt(jnp.mean(x * x, axis=-1, keepdims=True) + RMS_EPS) * w


def _silu(g):
    return g * (1.0 / (1.0 + jnp.exp(-g)))


def _dot(a, b):
    return jnp.dot(a, b, preferred_element_type=_F32)


def _dot_nt(a, b):
    return lax.dot_general(a, b, (((1,), (1,)), ((), ())), preferred_element_type=_F32)


def _const_spec(shape):
    nd = len(shape)
    return pl.BlockSpec(shape, lambda *_: (0,) * nd, pipeline_mode=pl.Buffered(1))


def _mem_kv_kernel(mem_ref, nw_ref, w_ref, o_ref):
    hm = _rms(mem_ref[0], nw_ref[0]).astype(_BF16)
    o_ref[0, 0] = _dot(hm, w_ref[0]).astype(_BF16)


def _mem_kv(mem_all, mem_norm_w, w_mem_kv_bf16):
    depth = w_mem_kv_bf16.shape[0]
    nb = mem_all.shape[0]
    return pl.pallas_call(
        _mem_kv_kernel,
        grid=(depth, nb),
        in_specs=[
            pl.BlockSpec((1, N_MEM, D_MODEL), lambda l, b: (b, 0, 0)),
            pl.BlockSpec((1, 1, D_MODEL), lambda l, b: (l, 0, 0)),
            pl.BlockSpec((1, D_MODEL, 2 * XATTN_WIDTH), lambda l, b: (l, 0, 0)),
        ],
        out_specs=pl.BlockSpec((1, 1, N_MEM, 2 * XATTN_WIDTH), lambda l, b: (l, b, 0, 0)),
        out_shape=jax.ShapeDtypeStruct((depth, nb, N_MEM, 2 * XATTN_WIDTH), _BF16),
        name="mem_kv",
    )(mem_all, mem_norm_w.reshape(depth, 1, D_MODEL), w_mem_kv_bf16)


def _memory_attention_into(y_scr, h, wq, wg, mkv):
    qm = _dot(h, wq)
    gm = _dot(h, wg)
    scale = XATTN_HEAD_DIM ** -0.5
    for hd in range(XATTN_HEADS):
        sl = slice(hd * XATTN_HEAD_DIM, (hd + 1) * XATTN_HEAD_DIM)
        q = qm[:, sl].astype(_BF16)
        k = mkv[:, sl]
        v = mkv[:, XATTN_WIDTH + hd * XATTN_HEAD_DIM: XATTN_WIDTH + (hd + 1) * XATTN_HEAD_DIM]
        s = _dot_nt(q, k) * scale
        e = jnp.exp(s - jnp.max(s, axis=-1, keepdims=True))
        o = _dot(e.astype(_BF16), v) / jnp.sum(e, axis=-1, keepdims=True)
        y_scr[:, MIX_WIDTH + hd * XATTN_HEAD_DIM: MIX_WIDTH + (hd + 1) * XATTN_HEAD_DIM] = (
            o * _silu(gm[:, sl])).astype(_BF16)


def _conv_layer_kernel(x_ref, xp_ref, xn_ref, nw_ref, win_ref, wout_ref, mkv_ref, cw_ref, cb_ref,
                       o_ref, y_scr):
    tm = x_ref.shape[1]
    halo = xp_ref.shape[1]
    i = pl.program_id(1)
    last = pl.num_programs(1) - 1
    x = x_ref[0]
    nw = nw_ref[...]
    h = _rms(x, nw).astype(_BF16)
    hp = jnp.where(i == 0, 0.0, _rms(xp_ref[0], nw)).astype(_BF16)
    hn = jnp.where(i == last, 0.0, _rms(xn_ref[0], nw)).astype(_BF16)
    h_ext = jnp.concatenate([hp, h, hn], axis=0)
    ext = tm + 2 * halo
    for j in range(MIX_WIDTH // CHUNK):
        c0 = j * CHUNK
        c = _dot(h_ext, win_ref[:, OFF_P1 + c0: OFF_P1 + c0 + CHUNK])
        u = _dot(h_ext, win_ref[:, OFF_P2 + c0: OFF_P2 + c0 + CHUNK])
        v = c * u
        v_prev = pltpu.roll(v, 1, 0)[halo:halo + tm]
        v_next = pltpu.roll(v, ext - 1, 0)[halo:halo + tm]
        v_cur = v[halo:halo + tm]
        cw = cw_ref[:, c0:c0 + CHUNK]
        conv = v_prev * cw[0:1] + v_cur * cw[1:2] + v_next * cw[2:3] + cb_ref[:, c0:c0 + CHUNK]
        bg = _dot(h, win_ref[:, c0:c0 + CHUNK])
        g = _dot(h, win_ref[:, OFF_GATE + c0: OFF_GATE + c0 + CHUNK])
        y_scr[:, c0:c0 + CHUNK] = (bg * conv * _silu(g)).astype(_BF16)
    _memory_attention_into(y_scr, h, win_ref[:, OFF_QMEM:OFF_QMEM + XATTN_WIDTH],
                           win_ref[:, OFF_GATE + MIX_WIDTH:], mkv_ref[0, 0])
    o_ref[0] = x + _dot(y_scr[...], wout_ref[...])


def _conv_layer(x, mkv, layer, b_off, nw, win, wout, cw, cb):
    bsz, seq, _ = x.shape
    tm = TOKEN_TILE
    halo = BF16_SUBLANES
    nt = seq // tm
    per = tm // halo
    return pl.pallas_call(
        _conv_layer_kernel,
        grid=(bsz, nt),
        in_specs=[
            pl.BlockSpec((1, tm, D_MODEL), lambda b, i: (b, i, 0)),
            pl.BlockSpec((1, halo, D_MODEL), lambda b, i: (b, jnp.maximum(i * per - 1, 0), 0)),
            pl.BlockSpec((1, halo, D_MODEL), lambda b, i: (b, jnp.minimum((i + 1) * per, nt * per - 1), 0)),
            _const_spec((1, D_MODEL)),
            _const_spec(win.shape),
            _const_spec(wout.shape),
            pl.BlockSpec((1, 1, N_MEM, 2 * XATTN_WIDTH), lambda b, i: (layer, b + b_off, 0, 0)),
            _const_spec(cw.shape),
            _const_spec(cb.shape),
        ],
        out_specs=pl.BlockSpec((1, tm, D_MODEL), lambda b, i: (b, i, 0)),
        out_shape=jax.ShapeDtypeStruct(x.shape, _F32),
        scratch_shapes=[pltpu.VMEM((tm, BRANCH_WIDTH), _BF16)],
        compiler_params=pltpu.CompilerParams(
            dimension_semantics=("arbitrary", "arbitrary"), vmem_limit_bytes=VMEM_LIMIT),
        name="conv_layer",
    )(x, x, x, nw, win, wout, mkv, cw, cb)


def _qkv_kernel(x_ref, nw_ref, wq_ref, wkt_ref, wv_ref, q_ref, kt_ref, v_ref):
    h = _rms(x_ref[0], nw_ref[...]).astype(_BF16)
    q_ref[0] = (_dot(h, wq_ref[...]) * (NA_HEAD_DIM ** -0.5)).astype(_BF16)
    kt_ref[0] = _dot_nt(wkt_ref[...], h).astype(_BF16)
    v_ref[0] = _dot(h, wv_ref[...]).astype(_BF16)


def _qkv(x, nw, wq, wkt, wv):
    bsz, seq, _ = x.shape
    tm = TOKEN_TILE
    return pl.pallas_call(
        _qkv_kernel,
        grid=(bsz, seq // tm),
        in_specs=[
            pl.BlockSpec((1, tm, D_MODEL), lambda b, i: (b, i, 0)),
            _const_spec((1, D_MODEL)),
            _const_spec(wq.shape),
            _const_spec(wkt.shape),
            _const_spec(wv.shape),
        ],
        out_specs=[
            pl.BlockSpec((1, tm, MIX_WIDTH), lambda b, i: (b, i, 0)),
            pl.BlockSpec((1, MIX_WIDTH, tm), lambda b, i: (b, 0, i)),
            pl.BlockSpec((1, tm, MIX_WIDTH), lambda b, i: (b, i, 0)),
        ],
        out_shape=[
            jax.ShapeDtypeStruct((bsz, seq, MIX_WIDTH), _BF16),
            jax.ShapeDtypeStruct((bsz, MIX_WIDTH, seq), _BF16),
            jax.ShapeDtypeStruct((bsz, seq, MIX_WIDTH), _BF16),
        ],
        compiler_params=pltpu.CompilerParams(
            dimension_semantics=("arbitrary", "arbitrary"), vmem_limit_bytes=VMEM_LIMIT),
        name="qkv_proj",
    )(x, nw, wq, wkt, wv)


def _na_bias_table(rpb):
    c = np.arange(GRID_W)[:, None]
    kc = np.arange(GRID_W)[None, :]
    cstart = np.clip(c - NA_WIN_W // 2, 0, GRID_W - NA_WIN_W)
    valid = (kc >= cstart) & (kc < cstart + NA_WIN_W)
    dx = np.clip(kc - c + NA_WIN_W - 1, 0, 2 * NA_WIN_W - 2)
    tab = rpb[:, :, dx]
    tab = jnp.where(jnp.asarray(valid)[None, None], tab, NEG_INF)
    tab = jnp.transpose(tab, (0, 2, 1, 3)).reshape(NA_HEADS, GRID_W, (2 * NA_WIN_H - 1) * GRID_W)
    tab = jnp.pad(tab, ((0, 0), (0, 0), (GRID_W, 0)), constant_values=NEG_INF)
    return tab.reshape(NA_HEAD_PAIRS, 2 * GRID_W, 2 * NA_WIN_H * GRID_W).astype(_F32)


def _na_kernel(q_ref, kp_ref, km_ref, kn_ref, vp_ref, vm_ref, vn_ref, bias_ref, o_ref,
               kbuf, kshift, vbuf):
    i = pl.program_id(1)
    last = pl.num_programs(1) - 1
    kbuf[:, 0:NA_HALO] = kp_ref[0]
    kbuf[:, NA_HALO:NA_HALO + NA_TILE] = km_ref[0]
    kbuf[:, NA_HALO + NA_TILE:] = kn_ref[0]
    vbuf[0:NA_HALO] = vp_ref[0]
    vbuf[NA_HALO:NA_HALO + NA_TILE] = vm_ref[0]
    vbuf[NA_HALO + NA_TILE:] = vn_ref[0]
    win_tokens = NA_TILE + 2 * NA_HALO

    def shift_body(hp, carry):
        r0 = pl.multiple_of(hp * LANES, LANES)
        kw = pltpu.bitcast(kbuf[pl.ds(r0, LANES), :], jnp.uint32)
        kshift[pl.ds(r0, LANES), :] = pltpu.bitcast(
            pltpu.roll(kw, win_tokens - GRID_W, 1), _BF16)
        return carry

    lax.fori_loop(0, NA_HEAD_PAIRS, shift_body, 0)

    lane = lax.broadcasted_iota(jnp.int32, (GRID_W, LANES), 1)
    low_half = lane < NA_HEAD_DIM

    def tile_variant(lo, hi):
        def hp_body(hp, carry):
            c0 = pl.multiple_of(hp * LANES, LANES)
            for qi in range(NA_TILE_ROWS):
                r0 = min(max(qi - NA_WIN_H // 2, lo), hi)
                tok = NA_HALO + r0 * GRID_W
                dy0 = NA_WIN_H - 1 - qi + r0
                rows = slice(qi * GRID_W, (qi + 1) * GRID_W)
                q2 = q_ref[0, rows, pl.ds(c0, LANES)]
                zero = jnp.zeros_like(q2)
                qs = jnp.concatenate([jnp.where(low_half, q2, zero),
                                      jnp.where(low_half, zero, q2)], axis=0)
                if tok % LANES == 0:
                    kw = kbuf[pl.ds(c0, LANES), tok:tok + NA_KEYS]
                else:
                    kw = kshift[pl.ds(c0, LANES), tok - GRID_W:tok - GRID_W + NA_KEYS]
                if dy0 == NA_WIN_H // 2 - 1:
                    bias = bias_ref[hp, :, 4 * GRID_W:4 * GRID_W + NA_KEYS]
                else:
                    shift = ((NA_WIN_H // 2 - 1 - dy0) * GRID_W) % (2 * NA_WIN_H * GRID_W)
                    bias = pltpu.roll(bias_ref[hp], shift, 1)[:, 4 * GRID_W:4 * GRID_W + NA_KEYS]
                s = _dot(qs, kw) + bias
                e = jnp.exp(s - jnp.max(s, axis=-1, keepdims=True))
                den = jnp.sum(e, axis=-1, keepdims=True)
                vw = vbuf[tok:tok + NA_KEYS, pl.ds(c0, LANES)]
                o = _dot(e.astype(_BF16), vw) / den
                o_ref[0, rows, pl.ds(c0, LANES)] = jnp.where(
                    low_half, o[0:GRID_W], o[GRID_W:]).astype(o_ref.dtype)
            return carry

        lax.fori_loop(0, NA_HEAD_PAIRS, hp_body, 0)

    big = NA_TILE_ROWS

    @pl.when(i == 0)
    def _():
        tile_variant(0, big)

    @pl.when(i == last)
    def _():
        tile_variant(-big, 0)

    @pl.when(jnp.logical_and(i > 0, i < last))
    def _():
        tile_variant(-big, big)


def _na(q, kt, v, bias):
    bsz, seq, _ = q.shape
    nt = seq // NA_TILE
    assert nt >= 2 and seq % NA_TILE == 0
    per = NA_TILE // NA_HALO
    nh = seq // NA_HALO
    prev_idx = lambda i: jnp.maximum(i * per - 1, 0)
    next_idx = lambda i: jnp.minimum((i + 1) * per, nh - 1)
    win_tokens = NA_TILE + 2 * NA_HALO
    return pl.pallas_call(
        _na_kernel,
        grid=(bsz, nt),
        in_specs=[
            pl.BlockSpec((1, NA_TILE, MIX_WIDTH), lambda b, i: (b, i, 0)),
            pl.BlockSpec((1, MIX_WIDTH, NA_HALO), lambda b, i: (b, 0, prev_idx(i))),
            pl.BlockSpec((1, MIX_WIDTH, NA_TILE), lambda b, i: (b, 0, i)),
            pl.BlockSpec((1, MIX_WIDTH, NA_HALO), lambda b, i: (b, 0, next_idx(i))),
            pl.BlockSpec((1, NA_HALO, MIX_WIDTH), lambda b, i: (b, prev_idx(i), 0)),
            pl.BlockSpec((1, NA_TILE, MIX_WIDTH), lambda b, i: (b, i, 0)),
            pl.BlockSpec((1, NA_HALO, MIX_WIDTH), lambda b, i: (b, next_idx(i), 0)),
            _const_spec(bias.shape),
        ],
        out_specs=pl.BlockSpec((1, NA_TILE, MIX_WIDTH), lambda b, i: (b, i, 0)),
        out_shape=jax.ShapeDtypeStruct((bsz, seq, MIX_WIDTH), _BF16),
        scratch_shapes=[
            pltpu.VMEM((MIX_WIDTH, win_tokens), _BF16),
            pltpu.VMEM((MIX_WIDTH, win_tokens), _BF16),
            pltpu.VMEM((win_tokens, MIX_WIDTH), _BF16),
        ],
        compiler_params=pltpu.CompilerParams(
            dimension_semantics=("arbitrary", "arbitrary"), vmem_limit_bytes=VMEM_LIMIT),
        name="na_attention",
    )(q, kt, kt, kt, v, v, v, bias)


def _na_tail_kernel(x_ref, mix_ref, nw_ref, wqg_ref, wout_ref, mkv_ref, fw_ref, o_ref, y_scr):
    x = x_ref[0]
    h = _rms(x, nw_ref[...]).astype(_BF16)
    for j in range(MIX_WIDTH // CHUNK):
        c0 = j * CHUNK
        g = _dot(h, wqg_ref[:, XATTN_WIDTH + c0: XATTN_WIDTH + c0 + CHUNK])
        y_scr[:, c0:c0 + CHUNK] = (mix_ref[0, :, c0:c0 + CHUNK].astype(_F32) * _silu(g)).astype(_BF16)
    _memory_attention_into(y_scr, h, wqg_ref[:, 0:XATTN_WIDTH],
                           wqg_ref[:, XATTN_WIDTH + MIX_WIDTH:], mkv_ref[0, 0])
    o_ref[0] = _rms(x + _dot(y_scr[...], wout_ref[...]), fw_ref[...])


def _na_tail(x, mix, mkv, layer, b_off, nw, wqg, wout, fw):
    bsz, seq, _ = x.shape
    tm = TOKEN_TILE
    return pl.pallas_call(
        _na_tail_kernel,
        grid=(bsz, seq // tm),
        in_specs=[
            pl.BlockSpec((1, tm, D_MODEL), lambda b, i: (b, i, 0)),
            pl.BlockSpec((1, tm, MIX_WIDTH), lambda b, i: (b, i, 0)),
            _const_spec((1, D_MODEL)),
            _const_spec(wqg.shape),
            _const_spec(wout.shape),
            pl.BlockSpec((1, 1, N_MEM, 2 * XATTN_WIDTH), lambda b, i: (layer, b + b_off, 0, 0)),
            _const_spec((1, D_MODEL)),
        ],
        out_specs=pl.BlockSpec((1, tm, D_MODEL), lambda b, i: (b, i, 0)),
        out_shape=jax.ShapeDtypeStruct(x.shape, _F32),
        scratch_shapes=[pltpu.VMEM((tm, BRANCH_WIDTH), _BF16)],
        compiler_params=pltpu.CompilerParams(
            dimension_semantics=("arbitrary", "arbitrary"), vmem_limit_bytes=VMEM_LIMIT),
        name="na_tail",
    )(x, mix, nw, wqg, wout, mkv, fw)


def kernel(x_prompt, x_sample, mem_prompt, mem_sample, norm_w, w_in, w_out, mem_norm_w, w_mem_kv,
           conv_w, conv_b, na_rpb, final_norm_w):
    assert w_in.shape[0] == 2 and conv_w.shape[0] == 1 and na_rpb.shape[0] == 1
    win = w_in.astype(_BF16)
    wout = w_out.astype(_BF16)
    mem_all = jnp.concatenate([mem_prompt, mem_sample], axis=0)
    mkv = _mem_kv(mem_all, mem_norm_w, w_mem_kv.astype(_BF16))
    nw0 = norm_w[0].reshape(1, D_MODEL)
    nw1 = norm_w[1].reshape(1, D_MODEL)
    fw = final_norm_w.reshape(1, D_MODEL)
    cw = conv_w[0]
    cb = conv_b[0].reshape(1, MIX_WIDTH)
    wq = win[1, :, 0:MIX_WIDTH]
    wkt = win[1, :, OFF_P1:OFF_P2].T
    wv = win[1, :, OFF_P2:OFF_QMEM]
    wqg = win[1, :, OFF_QMEM:]
    bias = _na_bias_table(na_rpb[0])

    def trunk(x, b_off):
        x1 = _conv_layer(x, mkv, 0, b_off, nw0, win[0], wout[0], cw, cb)
        q, kt, v = _qkv(x1, nw1, wq, wkt, wv)
        mix = _na(q, kt, v, bias)
        return _na_tail(x1, mix, mkv, 1, b_off, nw1, wqg, wout[1], fw)

    return (trunk(x_prompt, 0), trunk(x_sample, mem_prompt.shape[0]))
```

```python
import functools

import numpy as np
import jax
import jax.numpy as jnp
from jax import lax
from jax.experimental import pallas as pl
from jax.experimental.pallas import tpu as pltpu

D_MODEL = 1024
GRID_W = 64
N_MEM = 256
MIX_WIDTH = 1536
XATTN_WIDTH = 512
BRANCH_WIDTH = MIX_WIDTH + XATTN_WIDTH
NA_HEAD_DIM = 64
NA_HEADS = MIX_WIDTH // NA_HEAD_DIM
NA_HEAD_PAIRS = NA_HEADS // 2
XATTN_HEADS = 4
XATTN_HEAD_DIM = XATTN_WIDTH // XATTN_HEADS
NA_WIN_H = 8
NA_WIN_W = 16
RMS_EPS = 1e-6
NEG_INF = -1e30

OFF_P1 = MIX_WIDTH
OFF_P2 = 2 * MIX_WIDTH
OFF_QMEM = 3 * MIX_WIDTH
OFF_GATE = 3 * MIX_WIDTH + XATTN_WIDTH

LANES = 128
BF16_SUBLANES = 16
TOKEN_TILE = 512
CHUNK = 512
NA_TILE_ROWS = 8
NA_TILE = NA_TILE_ROWS * GRID_W
NA_HALO = 4 * GRID_W
NA_KEYS = NA_WIN_H * GRID_W
VMEM_LIMIT = 56 * 1024 * 1024

_BF16 = jnp.bfloat16
_F32 = jnp.float32


def _rms(x, w):
    return x * lax.rsqrt(jnp.mean(x * x, axis=-1, keepdims=True) + RMS_EPS) * w


def _silu(g):
    return g * (1.0 / (1.0 + jnp.exp(-g)))


def _dot(a, b):
    return jnp.dot(a, b, preferred_element_type=_F32)


def _dot_nt(a, b):
    return lax.dot_general(a, b, (((1,), (1,)), ((), ())), preferred_element_type=_F32)


def _const_spec(shape):
    nd = len(shape)
    return pl.BlockSpec(shape, lambda *_: (0,) * nd, pipeline_mode=pl.Buffered(1))


def _mem_kv_kernel(mem_ref, nw_ref, w_ref, o_ref):
    hm = _rms(mem_ref[0], nw_ref[0]).astype(_BF16)
    o_ref[0, 0] = _dot(hm, w_ref[0]).astype(_BF16)


def _mem_kv(mem_all, mem_norm_w, w_mem_kv_bf16):
    depth = w_mem_kv_bf16.shape[0]
    nb = mem_all.shape[0]
    return pl.pallas_call(
        _mem_kv_kernel,
        grid=(depth, nb),
        in_specs=[
            pl.BlockSpec((1, N_MEM, D_MODEL), lambda l, b: (b, 0, 0)),
            pl.BlockSpec((1, 1, D_MODEL), lambda l, b: (l, 0, 0)),
            pl.BlockSpec((1, D_MODEL, 2 * XATTN_WIDTH), lambda l, b: (l, 0, 0)),
        ],
        out_specs=pl.BlockSpec((1, 1, N_MEM, 2 * XATTN_WIDTH), lambda l, b: (l, b, 0, 0)),
        out_shape=jax.ShapeDtypeStruct((depth, nb, N_MEM, 2 * XATTN_WIDTH), _BF16),
        name="mem_kv",
    )(mem_all, mem_norm_w.reshape(depth, 1, D_MODEL), w_mem_kv_bf16)


def _memory_attention_into(y_scr, h, wq, wg, mkv):
    qm = _dot(h, wq)
    gm = _dot(h, wg)
    scale = XATTN_HEAD_DIM ** -0.5
    for hd in range(XATTN_HEADS):
        sl = slice(hd * XATTN_HEAD_DIM, (hd + 1) * XATTN_HEAD_DIM)
        q = qm[:, sl].astype(_BF16)
        k = mkv[:, sl]
        v = mkv[:, XATTN_WIDTH + hd * XATTN_HEAD_DIM: XATTN_WIDTH + (hd + 1) * XATTN_HEAD_DIM]
        s = _dot_nt(q, k) * scale
        e = jnp.exp(s - jnp.max(s, axis=-1, keepdims=True))
        o = _dot(e.astype(_BF16), v) / jnp.sum(e, axis=-1, keepdims=True)
        y_scr[:, MIX_WIDTH + hd * XATTN_HEAD_DIM: MIX_WIDTH + (hd + 1) * XATTN_HEAD_DIM] = (
            o * _silu(gm[:, sl])).astype(_BF16)


def _conv_layer_kernel(x_ref, xp_ref, xn_ref, nw_ref, win_ref, wout_ref, mkv_ref, cw_ref, cb_ref,
                       o_ref, y_scr):
    tm = x_ref.shape[1]
    halo = xp_ref.shape[1]
    i = pl.program_id(1)
    last = pl.num_programs(1) - 1
    x = x_ref[0]
    nw = nw_ref[...]
    h = _rms(x, nw).astype(_BF16)
    hp = jnp.where(i == 0, 0.0, _rms(xp_ref[0], nw)).astype(_BF16)
    hn = jnp.where(i == last, 0.0, _rms(xn_ref[0], nw)).astype(_BF16)
    h_ext = jnp.concatenate([hp, h, hn], axis=0)
    ext = tm + 2 * halo
    for j in range(MIX_WIDTH // CHUNK):
        c0 = j * CHUNK
        c = _dot(h_ext, win_ref[:, OFF_P1 + c0: OFF_P1 + c0 + CHUNK])
        u = _dot(h_ext, win_ref[:, OFF_P2 + c0: OFF_P2 + c0 + CHUNK])
        v = c * u
        v_prev = pltpu.roll(v, 1, 0)[halo:halo + tm]
        v_next = pltpu.roll(v, ext - 1, 0)[halo:halo + tm]
        v_cur = v[halo:halo + tm]
        cw = cw_ref[:, c0:c0 + CHUNK]
        conv = v_prev * cw[0:1] + v_cur * cw[1:2] + v_next * cw[2:3] + cb_ref[:, c0:c0 + CHUNK]
        bg = _dot(h, win_ref[:, c0:c0 + CHUNK])
        g = _dot(h, win_ref[:, OFF_GATE + c0: OFF_GATE + c0 + CHUNK])
        y_scr[:, c0:c0 + CHUNK] = (bg * conv * _silu(g)).astype(_BF16)
    _memory_attention_into(y_scr, h, win_ref[:, OFF_QMEM:OFF_QMEM + XATTN_WIDTH],
                           win_ref[:, OFF_GATE + MIX_WIDTH:], mkv_ref[0, 0])
    o_ref[0] = x + _dot(y_scr[...], wout_ref[...])


def _conv_layer(x, mkv, layer, b_off, nw, win, wout, cw, cb):
    bsz, seq, _ = x.shape
    tm = TOKEN_TILE
    halo = BF16_SUBLANES
    nt = seq // tm
    per = tm // halo
    return pl.pallas_call(
        _conv_layer_kernel,
        grid=(bsz, nt),
        in_specs=[
            pl.BlockSpec((1, tm, D_MODEL), lambda b, i: (b, i, 0)),
            pl.BlockSpec((1, halo, D_MODEL), lambda b, i: (b, jnp.maximum(i * per - 1, 0), 0)),
            pl.BlockSpec((1, halo, D_MODEL), lambda b, i: (b, jnp.minimum((i + 1) * per, nt * per - 1), 0)),
            _const_spec((1, D_MODEL)),
            _const_spec(win.shape),
            _const_spec(wout.shape),
            pl.BlockSpec((1, 1, N_MEM, 2 * XATTN_WIDTH), lambda b, i: (layer, b + b_off, 0, 0)),
            _const_spec(cw.shape),
            _const_spec(cb.shape),
        ],
        out_specs=pl.BlockSpec((1, tm, D_MODEL), lambda b, i: (b, i, 0)),
        out_shape=jax.ShapeDtypeStruct(x.shape, _F32),
        scratch_shapes=[pltpu.VMEM((tm, BRANCH_WIDTH), _BF16)],
        compiler_params=pltpu.CompilerParams(
            dimension_semantics=("arbitrary", "arbitrary"), vmem_limit_bytes=VMEM_LIMIT),
        name="conv_layer",
    )(x, x, x, nw, win, wout, mkv, cw, cb)


def _qkv_kernel(x_ref, nw_ref, wq_ref, wkt_ref, wv_ref, q_ref, kt_ref, v_ref):
    h = _rms(x_ref[0], nw_ref[...]).astype(_BF16)
    q_ref[0] = (_dot(h, wq_ref[...]) * (NA_HEAD_DIM ** -0.5)).astype(_BF16)
    kt_ref[0] = _dot_nt(wkt_ref[...], h).astype(_BF16)
    v_ref[0] = _dot(h, wv_ref[...]).astype(_BF16)


def _qkv(x, nw, wq, wkt, wv):
    bsz, seq, _ = x.shape
    tm = TOKEN_TILE
    return pl.pallas_call(
        _qkv_kernel,
        grid=(bsz, seq // tm),
        in_specs=[
            pl.BlockSpec((1, tm, D_MODEL), lambda b, i: (b, i, 0)),
            _const_spec((1, D_MODEL)),
            _const_spec(wq.shape),
            _const_spec(wkt.shape),
            _const_spec(wv.shape),
        ],
        out_specs=[
            pl.BlockSpec((1, tm, MIX_WIDTH), lambda b, i: (b, i, 0)),
            pl.BlockSpec((1, MIX_WIDTH, tm), lambda b, i: (b, 0, i)),
            pl.BlockSpec((1, tm, MIX_WIDTH), lambda b, i: (b, i, 0)),
        ],
        out_shape=[
            jax.ShapeDtypeStruct((bsz, seq, MIX_WIDTH), _BF16),
            jax.ShapeDtypeStruct((bsz, MIX_WIDTH, seq), _BF16),
            jax.ShapeDtypeStruct((bsz, seq, MIX_WIDTH), _BF16),
        ],
        compiler_params=pltpu.CompilerParams(
            dimension_semantics=("arbitrary", "arbitrary"), vmem_limit_bytes=VMEM_LIMIT),
        name="qkv_proj",
    )(x, nw, wq, wkt, wv)


def _na_bias_table(rpb):
    c = np.arange(GRID_W)[:, None]
    kc = np.arange(GRID_W)[None, :]
    cstart = np.clip(c - NA_WIN_W // 2, 0, GRID_W - NA_WIN_W)
    valid = (kc >= cstart) & (kc < cstart + NA_WIN_W)
    dx = np.clip(kc - c + NA_WIN_W - 1, 0, 2 * NA_WIN_W - 2)
    tab = rpb[:, :, dx]
    tab = jnp.where(jnp.asarray(valid)[None, None], tab, NEG_INF)
    tab = jnp.transpose(tab, (0, 2, 1, 3)).reshape(NA_HEADS, GRID_W, (2 * NA_WIN_H - 1) * GRID_W)
    tab = jnp.pad(tab, ((0, 0), (0, 0), (GRID_W, 0)), constant_values=NEG_INF)
    return tab.reshape(NA_HEAD_PAIRS, 2 * GRID_W, 2 * NA_WIN_H * GRID_W).astype(_F32)


def _na_kernel(q_ref, kp_ref, km_ref, kn_ref, vp_ref, vm_ref, vn_ref, bias_ref, o_ref,
               kbuf, kshift, vbuf, s_scr):
    i = pl.program_id(1)
    last = pl.num_programs(1) - 1
    kbuf[:, 0:NA_HALO] = kp_ref[0]
    kbuf[:, NA_HALO:NA_HALO + NA_TILE] = km_ref[0]
    kbuf[:, NA_HALO + NA_TILE:] = kn_ref[0]
    vbuf[0:NA_HALO] = vp_ref[0]
    vbuf[NA_HALO:NA_HALO + NA_TILE] = vm_ref[0]
    vbuf[NA_HALO + NA_TILE:] = vn_ref[0]
    win_tokens = NA_TILE + 2 * NA_HALO

    lane = lax.broadcasted_iota(jnp.int32, (GRID_W, LANES), 1)
    low_half = lane < NA_HEAD_DIM

    def tile_variant(lo, hi):
        def hp_body(hp, carry):
            c0 = pl.multiple_of(hp * LANES, LANES)
            kw32 = pltpu.bitcast(kbuf[pl.ds(c0, LANES), :], jnp.uint32)
            kshift[...] = pltpu.bitcast(pltpu.roll(kw32, win_tokens - GRID_W, 1), _BF16)
            toks = []
            maxes = []
            for qi in range(NA_TILE_ROWS):
                r0 = min(max(qi - NA_WIN_H // 2, lo), hi)
                tok = NA_HALO + r0 * GRID_W
                dy0 = NA_WIN_H - 1 - qi + r0
                rows = slice(qi * GRID_W, (qi + 1) * GRID_W)
                q2 = q_ref[0, rows, pl.ds(c0, LANES)]
                zero = jnp.zeros_like(q2)
                qs = jnp.concatenate([jnp.where(low_half, q2, zero),
                                      jnp.where(low_half, zero, q2)], axis=0)
                if tok % LANES == 0:
                    kw = kbuf[pl.ds(c0, LANES), tok:tok + NA_KEYS]
                else:
                    kw = kshift[:, tok - GRID_W:tok - GRID_W + NA_KEYS]
                if dy0 == NA_WIN_H // 2 - 1:
                    bias = bias_ref[hp, :, 4 * GRID_W:4 * GRID_W + NA_KEYS]
                else:
                    shift = ((NA_WIN_H // 2 - 1 - dy0) * GRID_W) % (2 * NA_WIN_H * GRID_W)
                    bias = pltpu.roll(bias_ref[hp], shift, 1)[:, 4 * GRID_W:4 * GRID_W + NA_KEYS]
                s = _dot(qs, kw) + bias
                s_scr[qi] = s
                maxes.append(jnp.max(s, axis=-1, keepdims=True))
                toks.append(tok)
            for qi in range(NA_TILE_ROWS):
                tok = toks[qi]
                rows = slice(qi * GRID_W, (qi + 1) * GRID_W)
                e = jnp.exp(s_scr[qi] - maxes[qi])
                den = jnp.sum(e, axis=-1, keepdims=True)
                vw = vbuf[tok:tok + NA_KEYS, pl.ds(c0, LANES)]
                o = _dot(e.astype(_BF16), vw) / den
                o_ref[0, rows, pl.ds(c0, LANES)] = jnp.where(
                    low_half, o[0:GRID_W], o[GRID_W:]).astype(o_ref.dtype)
            return carry

        lax.fori_loop(0, NA_HEAD_PAIRS, hp_body, 0)

    big = NA_TILE_ROWS

    @pl.when(i == 0)
    def _():
        tile_variant(0, big)

    @pl.when(i == last)
    def _():
        tile_variant(-big, 0)

    @pl.when(jnp.logical_and(i > 0, i < last))
    def _():
        tile_variant(-big, big)


def _na(q, kt, v, bias):
    bsz, seq, _ = q.shape
    nt = seq // NA_TILE
    assert nt >= 2 and seq % NA_TILE == 0
    per = NA_TILE // NA_HALO
    nh = seq // NA_HALO
    prev_idx = lambda i: jnp.maximum(i * per - 1, 0)
    next_idx = lambda i: jnp.minimum((i + 1) * per, nh - 1)
    win_tokens = NA_TILE + 2 * NA_HALO
    return pl.pallas_call(
        _na_kernel,
        grid=(bsz, nt),
        in_specs=[
            pl.BlockSpec((1, NA_TILE, MIX_WIDTH), lambda b, i: (b, i, 0)),
            pl.BlockSpec((1, MIX_WIDTH, NA_HALO), lambda b, i: (b, 0, prev_idx(i))),
            pl.BlockSpec((1, MIX_WIDTH, NA_TILE), lambda b, i: (b, 0, i)),
            pl.BlockSpec((1, MIX_WIDTH, NA_HALO), lambda b, i: (b, 0, next_idx(i))),
            pl.BlockSpec((1, NA_HALO, MIX_WIDTH), lambda b, i: (b, prev_idx(i), 0)),
            pl.BlockSpec((1, NA_TILE, MIX_WIDTH), lambda b, i: (b, i, 0)),
            pl.BlockSpec((1, NA_HALO, MIX_WIDTH), lambda b, i: (b, next_idx(i), 0)),
            _const_spec(bias.shape),
        ],
        out_specs=pl.BlockSpec((1, NA_TILE, MIX_WIDTH), lambda b, i: (b, i, 0)),
        out_shape=jax.ShapeDtypeStruct((bsz, seq, MIX_WIDTH), _BF16),
        scratch_shapes=[
            pltpu.VMEM((MIX_WIDTH, win_tokens), _BF16),
            pltpu.VMEM((LANES, win_tokens), _BF16),
            pltpu.VMEM((win_tokens, MIX_WIDTH), _BF16),
            pltpu.VMEM((NA_TILE_ROWS, 2 * GRID_W, NA_KEYS), _F32),
        ],
        compiler_params=pltpu.CompilerParams(
            dimension_semantics=("arbitrary", "arbitrary"), vmem_limit_bytes=VMEM_LIMIT),
        name="na_attention",
    )(q, kt, kt, kt, v, v, v, bias)


def _na_tail_kernel(x_ref, mix_ref, nw_ref, wqg_ref, wout_ref, mkv_ref, fw_ref, o_ref, y_scr):
    x = x_ref[0]
    h = _rms(x, nw_ref[...]).astype(_BF16)
    for j in range(MIX_WIDTH // CHUNK):
        c0 = j * CHUNK
        g = _dot(h, wqg_ref[:, XATTN_WIDTH + c0: XATTN_WIDTH + c0 + CHUNK])
        y_scr[:, c0:c0 + CHUNK] = (mix_ref[0, :, c0:c0 + CHUNK].astype(_F32) * _silu(g)).astype(_BF16)
    _memory_attention_into(y_scr, h, wqg_ref[:, 0:XATTN_WIDTH],
                           wqg_ref[:, XATTN_WIDTH + MIX_WIDTH:], mkv_ref[0, 0])
    o_ref[0] = _rms(x + _dot(y_scr[...], wout_ref[...]), fw_ref[...])


def _na_tail(x, mix, mkv, layer, b_off, nw, wqg, wout, fw):
    bsz, seq, _ = x.shape
    tm = TOKEN_TILE
    return pl.pallas_call(
        _na_tail_kernel,
        grid=(bsz, seq // tm),
        in_specs=[
            pl.BlockSpec((1, tm, D_MODEL), lambda b, i: (b, i, 0)),
            pl.BlockSpec((1, tm, MIX_WIDTH), lambda b, i: (b, i, 0)),
            _const_spec((1, D_MODEL)),
            _const_spec(wqg.shape),
            _const_spec(wout.shape),
            pl.BlockSpec((1, 1, N_MEM, 2 * XATTN_WIDTH), lambda b, i: (layer, b + b_off, 0, 0)),
            _const_spec((1, D_MODEL)),
        ],
        out_specs=pl.BlockSpec((1, tm, D_MODEL), lambda b, i: (b, i, 0)),
        out_shape=jax.ShapeDtypeStruct(x.shape, _F32),
        scratch_shapes=[pltpu.VMEM((tm, BRANCH_WIDTH), _BF16)],
        compiler_params=pltpu.CompilerParams(
            dimension_semantics=("arbitrary", "arbitrary"), vmem_limit_bytes=VMEM_LIMIT),
        name="na_tail",
    )(x, mix, nw, wqg, wout, mkv, fw)


def kernel(x_prompt, x_sample, mem_prompt, mem_sample, norm_w, w_in, w_out, mem_norm_w, w_mem_kv,
           conv_w, conv_b, na_rpb, final_norm_w):
    assert w_in.shape[0] == 2 and conv_w.shape[0] == 1 and na_rpb.shape[0] == 1
    win = w_in.astype(_BF16)
    wout = w_out.astype(_BF16)
    mem_all = jnp.concatenate([mem_prompt, mem_sample], axis=0)
    mkv = _mem_kv(mem_all, mem_norm_w, w_mem_kv.astype(_BF16))
    nw0 = norm_w[0].reshape(1, D_MODEL)
    nw1 = norm_w[1].reshape(1, D_MODEL)
    fw = final_norm_w.reshape(1, D_MODEL)
    cw = conv_w[0]
    cb = conv_b[0].reshape(1, MIX_WIDTH)
    wq = win[1, :, 0:MIX_WIDTH]
    wkt = win[1, :, OFF_P1:OFF_P2].T
    wv = win[1, :, OFF_P2:OFF_QMEM]
    wqg = win[1, :, OFF_QMEM:]
    bias = _na_bias_table(na_rpb[0])

    def trunk(x, b_off):
        x1 = _conv_layer(x, mkv, 0, b_off, nw0, win[0], wout[0], cw, cb)
        q, kt, v = _qkv(x1, nw1, wq, wkt, wv)
        mix = _na(q, kt, v, bias)
        return _na_tail(x1, mix, mkv, 1, b_off, nw1, wqg, wout[1], fw)

    return (trunk(x_prompt, 0), trunk(x_sample, mem_prompt.shape[0]))
```

```python
import functools

import numpy as np
import jax
import jax.numpy as jnp
from jax import lax
from jax.experimental import pallas as pl
from jax.experimental.pallas import tpu as pltpu

D_MODEL = 1024
GRID_W = 64
N_MEM = 256
MIX_WIDTH = 1536
XATTN_WIDTH = 512
BRANCH_WIDTH = MIX_WIDTH + XATTN_WIDTH
NA_HEAD_DIM = 64
NA_HEADS = MIX_WIDTH // NA_HEAD_DIM
NA_HEAD_PAIRS = NA_HEADS // 2
XATTN_HEADS = 4
XATTN_HEAD_DIM = XATTN_WIDTH // XATTN_HEADS
NA_WIN_H = 8
NA_WIN_W = 16
RMS_EPS = 1e-6
NEG_INF = -1e30

OFF_P1 = MIX_WIDTH
OFF_P2 = 2 * MIX_WIDTH
OFF_QMEM = 3 * MIX_WIDTH
OFF_GATE = 3 * MIX_WIDTH + XATTN_WIDTH

LANES = 128
BF16_SUBLANES = 16
TOKEN_TILE = 512
CHUNK = 512
NA_TILE_ROWS = 8
NA_TILE = NA_TILE_ROWS * GRID_W
NA_HALO = 4 * GRID_W
NA_KEYS = NA_WIN_H * GRID_W
NA_HP_UNROLL = 4
VMEM_LIMIT = 56 * 1024 * 1024

_BF16 = jnp.bfloat16
_F32 = jnp.float32


def _rms(x, w):
    return x * lax.rsqrt(jnp.mean(x * x, axis=-1, keepdims=True) + RMS_EPS) * w


def _silu(g):
    return g * (1.0 / (1.0 + jnp.exp(-g)))


def _dot(a, b):
    return jnp.dot(a, b, preferred_element_type=_F32)


def _dot_nt(a, b):
    return lax.dot_general(a, b, (((1,), (1,)), ((), ())), preferred_element_type=_F32)


def _const_spec(shape):
    nd = len(shape)
    return pl.BlockSpec(shape, lambda *_: (0,) * nd, pipeline_mode=pl.Buffered(1))


def _mem_kv_kernel(mem_ref, nw_ref, w_ref, o_ref):
    hm = _rms(mem_ref[0], nw_ref[0]).astype(_BF16)
    o_ref[0, 0] = _dot(hm, w_ref[0]).astype(_BF16)


def _mem_kv(mem_all, mem_norm_w, w_mem_kv_bf16):
    depth = w_mem_kv_bf16.shape[0]
    nb = mem_all.shape[0]
    return pl.pallas_call(
        _mem_kv_kernel,
        grid=(depth, nb),
        in_specs=[
            pl.BlockSpec((1, N_MEM, D_MODEL), lambda l, b: (b, 0, 0)),
            pl.BlockSpec((1, 1, D_MODEL), lambda l, b: (l, 0, 0)),
            pl.BlockSpec((1, D_MODEL, 2 * XATTN_WIDTH), lambda l, b: (l, 0, 0)),
        ],
        out_specs=pl.BlockSpec((1, 1, N_MEM, 2 * XATTN_WIDTH), lambda l, b: (l, b, 0, 0)),
        out_shape=jax.ShapeDtypeStruct((depth, nb, N_MEM, 2 * XATTN_WIDTH), _BF16),
        name="mem_kv",
    )(mem_all, mem_norm_w.reshape(depth, 1, D_MODEL), w_mem_kv_bf16)


def _memory_attention_into(y_scr, h, wq, wg, mkv):
    qm = _dot(h, wq)
    gm = _dot(h, wg)
    scale = XATTN_HEAD_DIM ** -0.5
    for hd in range(XATTN_HEADS):
        sl = slice(hd * XATTN_HEAD_DIM, (hd + 1) * XATTN_HEAD_DIM)
        q = qm[:, sl].astype(_BF16)
        k = mkv[:, sl]
        v = mkv[:, XATTN_WIDTH + hd * XATTN_HEAD_DIM: XATTN_WIDTH + (hd + 1) * XATTN_HEAD_DIM]
        s = _dot_nt(q, k) * scale
        e = jnp.exp(s - jnp.max(s, axis=-1, keepdims=True))
        o = _dot(e.astype(_BF16), v) / jnp.sum(e, axis=-1, keepdims=True)
        y_scr[:, MIX_WIDTH + hd * XATTN_HEAD_DIM: MIX_WIDTH + (hd + 1) * XATTN_HEAD_DIM] = (
            o * _silu(gm[:, sl])).astype(_BF16)


def _conv_layer_kernel(x_ref, xp_ref, xn_ref, nw_ref, win_ref, wout_ref, mkv_ref, cw_ref, cb_ref,
                       o_ref, y_scr, *, n_tiles):
    tm = x_ref.shape[1]
    halo = xp_ref.shape[1]
    i = pl.program_id(1)
    last = n_tiles - 1
    x = x_ref[0]
    nw = nw_ref[...]
    h = _rms(x, nw).astype(_BF16)
    hp = jnp.where(i == 0, 0.0, _rms(xp_ref[0], nw)).astype(_BF16)
    hn = jnp.where(i == last, 0.0, _rms(xn_ref[0], nw)).astype(_BF16)
    h_ext = jnp.concatenate([hp, h, hn], axis=0)
    ext = tm + 2 * halo
    for j in range(MIX_WIDTH // CHUNK):
        c0 = j * CHUNK
        c = _dot(h_ext, win_ref[:, OFF_P1 + c0: OFF_P1 + c0 + CHUNK])
        u = _dot(h_ext, win_ref[:, OFF_P2 + c0: OFF_P2 + c0 + CHUNK])
        v = c * u
        v_prev = pltpu.roll(v, 1, 0)[halo:halo + tm]
        v_next = pltpu.roll(v, ext - 1, 0)[halo:halo + tm]
        v_cur = v[halo:halo + tm]
        cw = cw_ref[:, c0:c0 + CHUNK]
        conv = v_prev * cw[0:1] + v_cur * cw[1:2] + v_next * cw[2:3] + cb_ref[:, c0:c0 + CHUNK]
        bg = _dot(h, win_ref[:, c0:c0 + CHUNK])
        g = _dot(h, win_ref[:, OFF_GATE + c0: OFF_GATE + c0 + CHUNK])
        y_scr[:, c0:c0 + CHUNK] = (bg * conv * _silu(g)).astype(_BF16)
    _memory_attention_into(y_scr, h, win_ref[:, OFF_QMEM:OFF_QMEM + XATTN_WIDTH],
                           win_ref[:, OFF_GATE + MIX_WIDTH:], mkv_ref[0, 0])
    o_ref[0] = x + _dot(y_scr[...], wout_ref[...])


def _conv_layer(x, mkv, layer, b_off, nw, win, wout, cw, cb):
    bsz, seq, _ = x.shape
    tm = TOKEN_TILE
    halo = BF16_SUBLANES
    nt = seq // tm
    per = tm // halo
    return pl.pallas_call(
        functools.partial(_conv_layer_kernel, n_tiles=nt),
        grid=(bsz, nt),
        in_specs=[
            pl.BlockSpec((1, tm, D_MODEL), lambda b, i: (b, i, 0)),
            pl.BlockSpec((1, halo, D_MODEL), lambda b, i: (b, jnp.maximum(i * per - 1, 0), 0)),
            pl.BlockSpec((1, halo, D_MODEL), lambda b, i: (b, jnp.minimum((i + 1) * per, nt * per - 1), 0)),
            _const_spec((1, D_MODEL)),
            _const_spec(win.shape),
            _const_spec(wout.shape),
            pl.BlockSpec((1, 1, N_MEM, 2 * XATTN_WIDTH), lambda b, i: (layer, b + b_off, 0, 0)),
            _const_spec(cw.shape),
            _const_spec(cb.shape),
        ],
        out_specs=pl.BlockSpec((1, tm, D_MODEL), lambda b, i: (b, i, 0)),
        out_shape=jax.ShapeDtypeStruct(x.shape, _F32),
        scratch_shapes=[pltpu.VMEM((tm, BRANCH_WIDTH), _BF16)],
        compiler_params=pltpu.CompilerParams(
            dimension_semantics=("arbitrary", "arbitrary"), vmem_limit_bytes=VMEM_LIMIT),
        name="conv_layer",
    )(x, x, x, nw, win, wout, mkv, cw, cb)


def _qkv_kernel(x_ref, nw_ref, wq_ref, wkt_ref, wv_ref, q_ref, kt_ref, v_ref):
    h = _rms(x_ref[0], nw_ref[...]).astype(_BF16)
    q_ref[0] = (_dot(h, wq_ref[...]) * (NA_HEAD_DIM ** -0.5)).astype(_BF16)
    kt_ref[0] = _dot_nt(wkt_ref[...], h).astype(_BF16)
    v_ref[0] = _dot(h, wv_ref[...]).astype(_BF16)


def _qkv(x, nw, wq, wkt, wv):
    bsz, seq, _ = x.shape
    tm = TOKEN_TILE
    return pl.pallas_call(
        _qkv_kernel,
        grid=(bsz, seq // tm),
        in_specs=[
            pl.BlockSpec((1, tm, D_MODEL), lambda b, i: (b, i, 0)),
            _const_spec((1, D_MODEL)),
            _const_spec(wq.shape),
            _const_spec(wkt.shape),
            _const_spec(wv.shape),
        ],
        out_specs=[
            pl.BlockSpec((1, tm, MIX_WIDTH), lambda b, i: (b, i, 0)),
            pl.BlockSpec((1, MIX_WIDTH, tm), lambda b, i: (b, 0, i)),
            pl.BlockSpec((1, tm, MIX_WIDTH), lambda b, i: (b, i, 0)),
        ],
        out_shape=[
            jax.ShapeDtypeStruct((bsz, seq, MIX_WIDTH), _BF16),
            jax.ShapeDtypeStruct((bsz, MIX_WIDTH, seq), _BF16),
            jax.ShapeDtypeStruct((bsz, seq, MIX_WIDTH), _BF16),
        ],
        compiler_params=pltpu.CompilerParams(
            dimension_semantics=("arbitrary", "arbitrary"), vmem_limit_bytes=VMEM_LIMIT),
        name="qkv_proj",
    )(x, nw, wq, wkt, wv)


def _na_bias_table(rpb):
    c = np.arange(GRID_W)[:, None]
    kc = np.arange(GRID_W)[None, :]
    cstart = np.clip(c - NA_WIN_W // 2, 0, GRID_W - NA_WIN_W)
    valid = (kc >= cstart) & (kc < cstart + NA_WIN_W)
    dx = np.clip(kc - c + NA_WIN_W - 1, 0, 2 * NA_WIN_W - 2)
    tab = rpb[:, :, dx]
    tab = jnp.where(jnp.asarray(valid)[None, None], tab, NEG_INF)
    tab = jnp.transpose(tab, (0, 2, 1, 3)).reshape(NA_HEADS, GRID_W, (2 * NA_WIN_H - 1) * GRID_W)
    tab = jnp.pad(tab, ((0, 0), (0, 0), (GRID_W, 0)), constant_values=NEG_INF)
    return tab.reshape(NA_HEAD_PAIRS, 2 * GRID_W, 2 * NA_WIN_H * GRID_W).astype(_F32)


def _na_kernel(q_ref, kp_ref, km_ref, kn_ref, vp_ref, vm_ref, vn_ref, bias_ref, o_ref,
               kbuf, kshift, vbuf, s_scr, *, n_tiles):
    i = pl.program_id(1)
    last = n_tiles - 1
    kbuf[:, 0:NA_HALO] = kp_ref[0]
    kbuf[:, NA_HALO:NA_HALO + NA_TILE] = km_ref[0]
    kbuf[:, NA_HALO + NA_TILE:] = kn_ref[0]
    win_tokens = NA_TILE + 2 * NA_HALO

    @pl.when(jnp.logical_and(pl.program_id(0) == 0, i == 0))
    def _():
        vbuf[...] = jnp.ones(vbuf.shape, vbuf.dtype)

    for hp in range(NA_HEAD_PAIRS):
        src = slice(hp * LANES, (hp + 1) * LANES)
        dst = slice(2 * hp * LANES, (2 * hp + 1) * LANES)
        vbuf[0:NA_HALO, dst] = vp_ref[0, :, src]
        vbuf[NA_HALO:NA_HALO + NA_TILE, dst] = vm_ref[0, :, src]
        vbuf[NA_HALO + NA_TILE:, dst] = vn_ref[0, :, src]

    lane = lax.broadcasted_iota(jnp.int32, (GRID_W, LANES), 1)
    low_half = lane < NA_HEAD_DIM

    def tile_variant(lo, hi):
        def head_pair(hp, u):
            c0 = pl.multiple_of(hp * LANES, LANES)
            kw32 = pltpu.bitcast(kbuf[pl.ds(c0, LANES), :], jnp.uint32)
            kshift[u] = pltpu.bitcast(pltpu.roll(kw32, win_tokens - GRID_W, 1), _BF16)
            toks = []
            maxes = []
            for qi in range(NA_TILE_ROWS):
                r0 = min(max(qi - NA_WIN_H // 2, lo), hi)
                tok = NA_HALO + r0 * GRID_W
                dy0 = NA_WIN_H - 1 - qi + r0
                rows = slice(qi * GRID_W, (qi + 1) * GRID_W)
                q2 = q_ref[0, rows, pl.ds(c0, LANES)]
                zero = jnp.zeros_like(q2)
                qs = jnp.concatenate([jnp.where(low_half, q2, zero),
                                      jnp.where(low_half, zero, q2)], axis=0)
                if tok % LANES == 0:
                    kw = kbuf[pl.ds(c0, LANES), tok:tok + NA_KEYS]
                else:
                    kw = kshift[u, :, tok - GRID_W:tok - GRID_W + NA_KEYS]
                b0 = (dy0 + 1) * GRID_W
                if b0 % LANES == 0:
                    bias = bias_ref[hp, :, b0:b0 + NA_KEYS]
                else:
                    wide = bias_ref[hp, :, b0 - GRID_W:b0 - GRID_W + NA_KEYS + LANES]
                    bias = pltpu.roll(wide, NA_KEYS + LANES - GRID_W, 1)[:, 0:NA_KEYS]
                s = _dot(qs, kw) + bias
                s_scr[u * NA_TILE_ROWS + qi] = s
                maxes.append(jnp.max(s, axis=-1, keepdims=True))
                toks.append(tok)
            for qi in range(NA_TILE_ROWS):
                tok = toks[qi]
                rows = slice(qi * GRID_W, (qi + 1) * GRID_W)
                e = jnp.exp(s_scr[u * NA_TILE_ROWS + qi] - maxes[qi])
                vw = vbuf[tok:tok + NA_KEYS, pl.ds(2 * c0, 2 * LANES)]
                od = _dot(e.astype(_BF16), vw)
                o = od[:, 0:LANES] / od[:, LANES:]
                o_ref[0, rows, pl.ds(c0, LANES)] = jnp.where(
                    low_half, o[0:GRID_W], o[GRID_W:]).astype(o_ref.dtype)

        def hp_body(j, carry):
            for u in range(NA_HP_UNROLL):
                head_pair(j * NA_HP_UNROLL + u, u)
            return carry

        lax.fori_loop(0, NA_HEAD_PAIRS // NA_HP_UNROLL, hp_body, 0)

    big = NA_TILE_ROWS

    @pl.when(i == 0)
    def _():
        tile_variant(0, big)

    @pl.when(i == last)
    def _():
        tile_variant(-big, 0)

    @pl.when(jnp.logical_and(i > 0, i < last))
    def _():
        tile_variant(-big, big)


def _na(q, kt, v, bias):
    bsz, seq, _ = q.shape
    nt = seq // NA_TILE
    assert nt >= 2 and seq % NA_TILE == 0
    per = NA_TILE // NA_HALO
    nh = seq // NA_HALO
    prev_idx = lambda i: jnp.maximum(i * per - 1, 0)
    next_idx = lambda i: jnp.minimum((i + 1) * per, nh - 1)
    win_tokens = NA_TILE + 2 * NA_HALO
    return pl.pallas_call(
        functools.partial(_na_kernel, n_tiles=nt),
        grid=(bsz, nt),
        in_specs=[
            pl.BlockSpec((1, NA_TILE, MIX_WIDTH), lambda b, i: (b, i, 0)),
            pl.BlockSpec((1, MIX_WIDTH, NA_HALO), lambda b, i: (b, 0, prev_idx(i))),
            pl.BlockSpec((1, MIX_WIDTH, NA_TILE), lambda b, i: (b, 0, i)),
            pl.BlockSpec((1, MIX_WIDTH, NA_HALO), lambda b, i: (b, 0, next_idx(i))),
            pl.BlockSpec((1, NA_HALO, MIX_WIDTH), lambda b, i: (b, prev_idx(i), 0)),
            pl.BlockSpec((1, NA_TILE, MIX_WIDTH), lambda b, i: (b, i, 0)),
            pl.BlockSpec((1, NA_HALO, MIX_WIDTH), lambda b, i: (b, next_idx(i), 0)),
            _const_spec(bias.shape),
        ],
        out_specs=pl.BlockSpec((1, NA_TILE, MIX_WIDTH), lambda b, i: (b, i, 0)),
        out_shape=jax.ShapeDtypeStruct((bsz, seq, MIX_WIDTH), _BF16),
        scratch_shapes=[
            pltpu.VMEM((MIX_WIDTH, win_tokens), _BF16),
            pltpu.VMEM((NA_HP_UNROLL, LANES, win_tokens), _BF16),
            pltpu.VMEM((win_tokens, 2 * MIX_WIDTH), _BF16),
            pltpu.VMEM((NA_HP_UNROLL * NA_TILE_ROWS, 2 * GRID_W, NA_KEYS), _F32),
        ],
        compiler_params=pltpu.CompilerParams(
            dimension_semantics=("arbitrary", "arbitrary"), vmem_limit_bytes=VMEM_LIMIT),
        name="na_attention",
    )(q, kt, kt, kt, v, v, v, bias)


def _na_tail_kernel(x_ref, mix_ref, nw_ref, wqg_ref, wout_ref, mkv_ref, fw_ref, o_ref, y_scr):
    x = x_ref[0]
    h = _rms(x, nw_ref[...]).astype(_BF16)
    for j in range(MIX_WIDTH // CHUNK):
        c0 = j * CHUNK
        g = _dot(h, wqg_ref[:, XATTN_WIDTH + c0: XATTN_WIDTH + c0 + CHUNK])
        y_scr[:, c0:c0 + CHUNK] = (mix_ref[0, :, c0:c0 + CHUNK].astype(_F32) * _silu(g)).astype(_BF16)
    _memory_attention_into(y_scr, h, wqg_ref[:, 0:XATTN_WIDTH],
                           wqg_ref[:, XATTN_WIDTH + MIX_WIDTH:], mkv_ref[0, 0])
    o_ref[0] = _rms(x + _dot(y_scr[...], wout_ref[...]), fw_ref[...])


def _na_tail(x, mix, mkv, layer, b_off, nw, wqg, wout, fw):
    bsz, seq, _ = x.shape
    tm = TOKEN_TILE
    return pl.pallas_call(
        _na_tail_kernel,
        grid=(bsz, seq // tm),
        in_specs=[
            pl.BlockSpec((1, tm, D_MODEL), lambda b, i: (b, i, 0)),
            pl.BlockSpec((1, tm, MIX_WIDTH), lambda b, i: (b, i, 0)),
            _const_spec((1, D_MODEL)),
            _const_spec(wqg.shape),
            _const_spec(wout.shape),
            pl.BlockSpec((1, 1, N_MEM, 2 * XATTN_WIDTH), lambda b, i: (layer, b + b_off, 0, 0)),
            _const_spec((1, D_MODEL)),
        ],
        out_specs=pl.BlockSpec((1, tm, D_MODEL), lambda b, i: (b, i, 0)),
        out_shape=jax.ShapeDtypeStruct(x.shape, _F32),
        scratch_shapes=[pltpu.VMEM((tm, BRANCH_WIDTH), _BF16)],
        compiler_params=pltpu.CompilerParams(
            dimension_semantics=("arbitrary", "arbitrary"), vmem_limit_bytes=VMEM_LIMIT),
        name="na_tail",
    )(x, mix, nw, wqg, wout, mkv, fw)


def kernel(x_prompt, x_sample, mem_prompt, mem_sample, norm_w, w_in, w_out, mem_norm_w, w_mem_kv,
           conv_w, conv_b, na_rpb, final_norm_w):
    assert w_in.shape[0] == 2 and conv_w.shape[0] == 1 and na_rpb.shape[0] == 1
    win = w_in.astype(_BF16)
    wout = w_out.astype(_BF16)
    mem_all = jnp.concatenate([mem_prompt, mem_sample], axis=0)
    mkv = _mem_kv(mem_all, mem_norm_w, w_mem_kv.astype(_BF16))
    nw0 = norm_w[0].reshape(1, D_MODEL)
    nw1 = norm_w[1].reshape(1, D_MODEL)
    fw = final_norm_w.reshape(1, D_MODEL)
    cw = conv_w[0]
    cb = conv_b[0].reshape(1, MIX_WIDTH)
    wq = win[1, :, 0:MIX_WIDTH]
    wkt = win[1, :, OFF_P1:OFF_P2].T
    wv = win[1, :, OFF_P2:OFF_QMEM]
    wqg = win[1, :, OFF_QMEM:]
    bias = _na_bias_table(na_rpb[0])

    def trunk(x, b_off):
        x1 = _conv_layer(x, mkv, 0, b_off, nw0, win[0], wout[0], cw, cb)
        q, kt, v = _qkv(x1, nw1, wq, wkt, wv)
        mix = _na(q, kt, v, bias)
        return _na_tail(x1, mix, mkv, 1, b_off, nw1, wqg, wout[1], fw)

    return (trunk(x_prompt, 0), trunk(x_sample, mem_prompt.shape[0]))
```

```python
import functools

import numpy as np
import jax
import jax.numpy as jnp
from jax import lax
from jax.experimental import pallas as pl
from jax.experimental.pallas import tpu as pltpu

D_MODEL = 1024
GRID_W = 64
N_MEM = 256
MIX_WIDTH = 1536
XATTN_WIDTH = 512
BRANCH_WIDTH = MIX_WIDTH + XATTN_WIDTH
NA_HEAD_DIM = 64
NA_HEADS = MIX_WIDTH // NA_HEAD_DIM
NA_HEAD_PAIRS = NA_HEADS // 2
XATTN_HEADS = 4
XATTN_HEAD_DIM = XATTN_WIDTH // XATTN_HEADS
NA_WIN_H = 8
NA_WIN_W = 16
RMS_EPS = 1e-6
NEG_INF = -1e30

OFF_P1 = MIX_WIDTH
OFF_P2 = 2 * MIX_WIDTH
OFF_QMEM = 3 * MIX_WIDTH
OFF_GATE = 3 * MIX_WIDTH + XATTN_WIDTH

LANES = 128
BF16_SUBLANES = 16
TOKEN_TILE = 512
CHUNK = 512
NA_TILE_ROWS = 8
NA_TILE = NA_TILE_ROWS * GRID_W
NA_HALO = 4 * GRID_W
NA_KEYS = NA_WIN_H * GRID_W
NA_HP_UNROLL = 4
NA_ROW_LAG = 6
VMEM_LIMIT = 56 * 1024 * 1024

_BF16 = jnp.bfloat16
_F32 = jnp.float32


def _rms(x, w):
    return x * lax.rsqrt(jnp.mean(x * x, axis=-1, keepdims=True) + RMS_EPS) * w


def _silu(g):
    return g * (1.0 / (1.0 + jnp.exp(-g)))


def _dot(a, b):
    return jnp.dot(a, b, preferred_element_type=_F32)


def _dot_nt(a, b):
    return lax.dot_general(a, b, (((1,), (1,)), ((), ())), preferred_element_type=_F32)


def _fixed_spec(block_shape, block_index):
    return pl.BlockSpec(block_shape, lambda *_: block_index, pipeline_mode=pl.Buffered(1))


def _const_spec(shape):
    return _fixed_spec(shape, (0,) * len(shape))


def _mem_kv_kernel(mem_ref, nw_ref, w_ref, o_ref):
    hm = _rms(mem_ref[0], nw_ref[0]).astype(_BF16)
    o_ref[0, 0] = _dot(hm, w_ref[0]).astype(_BF16)


def _mem_kv(mem_all, mem_norm_w, w_mem_kv_bf16):
    depth = w_mem_kv_bf16.shape[0]
    nb = mem_all.shape[0]
    return pl.pallas_call(
        _mem_kv_kernel,
        grid=(depth, nb),
        in_specs=[
            pl.BlockSpec((1, N_MEM, D_MODEL), lambda l, b: (b, 0, 0)),
            pl.BlockSpec((1, 1, D_MODEL), lambda l, b: (l, 0, 0)),
            pl.BlockSpec((1, D_MODEL, 2 * XATTN_WIDTH), lambda l, b: (l, 0, 0)),
        ],
        out_specs=pl.BlockSpec((1, 1, N_MEM, 2 * XATTN_WIDTH), lambda l, b: (l, b, 0, 0)),
        out_shape=jax.ShapeDtypeStruct((depth, nb, N_MEM, 2 * XATTN_WIDTH), _BF16),
        name="mem_kv",
    )(mem_all, mem_norm_w.reshape(depth, 1, D_MODEL), w_mem_kv_bf16)


def _memory_attention_into(y_scr, h, wq, wg, mkv):
    qm = _dot(h, wq)
    gm = _dot(h, wg)
    scale = XATTN_HEAD_DIM ** -0.5
    for hd in range(XATTN_HEADS):
        sl = slice(hd * XATTN_HEAD_DIM, (hd + 1) * XATTN_HEAD_DIM)
        q = qm[:, sl].astype(_BF16)
        k = mkv[:, sl]
        v = mkv[:, XATTN_WIDTH + hd * XATTN_HEAD_DIM: XATTN_WIDTH + (hd + 1) * XATTN_HEAD_DIM]
        s = _dot_nt(q, k) * scale
        e = jnp.exp(s - jnp.max(s, axis=-1, keepdims=True))
        o = _dot(e.astype(_BF16), v) / jnp.sum(e, axis=-1, keepdims=True)
        y_scr[:, MIX_WIDTH + hd * XATTN_HEAD_DIM: MIX_WIDTH + (hd + 1) * XATTN_HEAD_DIM] = (
            o * _silu(gm[:, sl])).astype(_BF16)


def _conv_layer_kernel(x_ref, xp_ref, xn_ref, nw_ref, win_ref, wout_ref, mkv_ref, cw_ref, cb_ref,
                       o_ref, y_scr, *, n_tiles):
    tm = x_ref.shape[1]
    halo = xp_ref.shape[1]
    i = pl.program_id(1)
    last = n_tiles - 1
    x = x_ref[0]
    nw = nw_ref[...]
    h = _rms(x, nw).astype(_BF16)
    hp = jnp.where(i == 0, 0.0, _rms(xp_ref[0], nw)).astype(_BF16)
    hn = jnp.where(i == last, 0.0, _rms(xn_ref[0], nw)).astype(_BF16)
    h_ext = jnp.concatenate([hp, h, hn], axis=0)
    ext = tm + 2 * halo
    win_ref = win_ref.at[0]
    wout_ref = wout_ref.at[0]
    for j in range(MIX_WIDTH // CHUNK):
        c0 = j * CHUNK
        c = _dot(h_ext, win_ref[:, OFF_P1 + c0: OFF_P1 + c0 + CHUNK])
        u = _dot(h_ext, win_ref[:, OFF_P2 + c0: OFF_P2 + c0 + CHUNK])
        v = c * u
        v_prev = pltpu.roll(v, 1, 0)[halo:halo + tm]
        v_next = pltpu.roll(v, ext - 1, 0)[halo:halo + tm]
        v_cur = v[halo:halo + tm]
        cw = cw_ref[:, c0:c0 + CHUNK]
        conv = v_prev * cw[0:1] + v_cur * cw[1:2] + v_next * cw[2:3] + cb_ref[:, c0:c0 + CHUNK]
        bg = _dot(h, win_ref[:, c0:c0 + CHUNK])
        g = _dot(h, win_ref[:, OFF_GATE + c0: OFF_GATE + c0 + CHUNK])
        y_scr[:, c0:c0 + CHUNK] = (bg * conv * _silu(g)).astype(_BF16)
    _memory_attention_into(y_scr, h, win_ref[:, OFF_QMEM:OFF_QMEM + XATTN_WIDTH],
                           win_ref[:, OFF_GATE + MIX_WIDTH:], mkv_ref[0, 0])
    o_ref[0] = x + _dot(y_scr[...], wout_ref[...])


def _conv_layer(x, mkv, layer, b_off, nw, win, wout, cw, cb):
    bsz, seq, _ = x.shape
    tm = TOKEN_TILE
    halo = BF16_SUBLANES
    nt = seq // tm
    per = tm // halo
    return pl.pallas_call(
        functools.partial(_conv_layer_kernel, n_tiles=nt),
        grid=(bsz, nt),
        in_specs=[
            pl.BlockSpec((1, tm, D_MODEL), lambda b, i: (b, i, 0)),
            pl.BlockSpec((1, halo, D_MODEL), lambda b, i: (b, jnp.maximum(i * per - 1, 0), 0)),
            pl.BlockSpec((1, halo, D_MODEL), lambda b, i: (b, jnp.minimum((i + 1) * per, nt * per - 1), 0)),
            _const_spec((1, D_MODEL)),
            _fixed_spec((1,) + win.shape[1:], (layer, 0, 0)),
            _fixed_spec((1,) + wout.shape[1:], (layer, 0, 0)),
            pl.BlockSpec((1, 1, N_MEM, 2 * XATTN_WIDTH), lambda b, i: (layer, b + b_off, 0, 0)),
            _const_spec(cw.shape),
            _const_spec(cb.shape),
        ],
        out_specs=pl.BlockSpec((1, tm, D_MODEL), lambda b, i: (b, i, 0)),
        out_shape=jax.ShapeDtypeStruct(x.shape, _F32),
        scratch_shapes=[pltpu.VMEM((tm, BRANCH_WIDTH), _BF16)],
        compiler_params=pltpu.CompilerParams(
            dimension_semantics=("arbitrary", "arbitrary"), vmem_limit_bytes=VMEM_LIMIT),
        name="conv_layer",
    )(x, x, x, nw, win, wout, mkv, cw, cb)


def _qkv_kernel(x_ref, nw_ref, wq_ref, wkt_ref, wv_ref, q_ref, kt_ref, v_ref):
    h = _rms(x_ref[0], nw_ref[...]).astype(_BF16)
    q_ref[0] = (_dot(h, wq_ref[0]) * (NA_HEAD_DIM ** -0.5)).astype(_BF16)
    kt_ref[0] = _dot_nt(wkt_ref[...], h).astype(_BF16)
    v_ref[0] = _dot(h, wv_ref[0]).astype(_BF16)


def _qkv(x, layer, nw, win, wkt):
    bsz, seq, _ = x.shape
    tm = TOKEN_TILE
    wblock = (1, D_MODEL, MIX_WIDTH)
    return pl.pallas_call(
        _qkv_kernel,
        grid=(bsz, seq // tm),
        in_specs=[
            pl.BlockSpec((1, tm, D_MODEL), lambda b, i: (b, i, 0)),
            _const_spec((1, D_MODEL)),
            _fixed_spec(wblock, (layer, 0, 0)),
            _const_spec(wkt.shape),
            _fixed_spec(wblock, (layer, 0, OFF_P2 // MIX_WIDTH)),
        ],
        out_specs=[
            pl.BlockSpec((1, tm, MIX_WIDTH), lambda b, i: (b, i, 0)),
            pl.BlockSpec((1, MIX_WIDTH, tm), lambda b, i: (b, 0, i)),
            pl.BlockSpec((1, tm, MIX_WIDTH), lambda b, i: (b, i, 0)),
        ],
        out_shape=[
            jax.ShapeDtypeStruct((bsz, seq, MIX_WIDTH), _BF16),
            jax.ShapeDtypeStruct((bsz, MIX_WIDTH, seq), _BF16),
            jax.ShapeDtypeStruct((bsz, seq, MIX_WIDTH), _BF16),
        ],
        compiler_params=pltpu.CompilerParams(
            dimension_semantics=("arbitrary", "arbitrary"), vmem_limit_bytes=VMEM_LIMIT),
        name="qkv_proj",
    )(x, nw, win, wkt, win)


def _na_bias_table(rpb):
    c = np.arange(GRID_W)[:, None]
    kc = np.arange(GRID_W)[None, :]
    cstart = np.clip(c - NA_WIN_W // 2, 0, GRID_W - NA_WIN_W)
    valid = (kc >= cstart) & (kc < cstart + NA_WIN_W)
    dx = kc - c + NA_WIN_W - 1
    select = ((np.arange(2 * NA_WIN_W - 1)[:, None, None] == dx[None]) & valid[None]).astype(np.float32)
    mask = np.full((GRID_W, 2 * NA_WIN_H, GRID_W), NEG_INF, np.float32)
    mask[:, 1:, :] = np.where(valid, 0.0, NEG_INF)[:, None, :]
    rpb_padded = jnp.pad(rpb.astype(_F32), ((0, 0), (1, 0), (0, 0)))
    tab = jnp.einsum("hyd,dck->hcyk", rpb_padded, jnp.asarray(select),
                     precision=lax.Precision.HIGHEST) + jnp.asarray(mask)[None]
    return tab.reshape(NA_HEAD_PAIRS, 2 * GRID_W, 2 * NA_WIN_H * GRID_W)


def _na_kernel(q_ref, kp_ref, km_ref, kn_ref, vp_ref, vm_ref, vn_ref, bias_ref, o_ref,
               kbuf, kshift, vbuf, s_scr, *, n_tiles):
    i = pl.program_id(1)
    last = n_tiles - 1
    kbuf[:, 0:NA_HALO] = kp_ref[0]
    kbuf[:, NA_HALO:NA_HALO + NA_TILE] = km_ref[0]
    kbuf[:, NA_HALO + NA_TILE:] = kn_ref[0]
    win_tokens = NA_TILE + 2 * NA_HALO

    @pl.when(jnp.logical_and(pl.program_id(0) == 0, i == 0))
    def _():
        vbuf[...] = jnp.ones(vbuf.shape, vbuf.dtype)

    for hp in range(NA_HEAD_PAIRS):
        src = slice(hp * LANES, (hp + 1) * LANES)
        dst = slice(2 * hp * LANES, (2 * hp + 1) * LANES)
        vbuf[0:NA_HALO, dst] = vp_ref[0, :, src]
        vbuf[NA_HALO:NA_HALO + NA_TILE, dst] = vm_ref[0, :, src]
        vbuf[NA_HALO + NA_TILE:, dst] = vn_ref[0, :, src]

    lane = lax.broadcasted_iota(jnp.int32, (GRID_W, LANES), 1)
    low_half = lane < NA_HEAD_DIM

    def tile_variant(lo, hi):
        def window(qi):
            r0 = min(max(qi - NA_WIN_H // 2, lo), hi)
            return NA_HALO + r0 * GRID_W, NA_WIN_H - 1 - qi + r0

        def shift_keys(hp, u):
            c0 = pl.multiple_of(hp * LANES, LANES)
            kw32 = pltpu.bitcast(kbuf[pl.ds(c0, LANES), :], jnp.uint32)
            kshift[u] = pltpu.bitcast(pltpu.roll(kw32, win_tokens - GRID_W, 1), _BF16)

        def scores(hp, u, qi):
            c0 = pl.multiple_of(hp * LANES, LANES)
            tok, dy0 = window(qi)
            rows = slice(qi * GRID_W, (qi + 1) * GRID_W)
            q2 = q_ref[0, rows, pl.ds(c0, LANES)]
            zero = jnp.zeros_like(q2)
            qs = jnp.concatenate([jnp.where(low_half, q2, zero),
                                  jnp.where(low_half, zero, q2)], axis=0)
            if tok % LANES == 0:
                kw = kbuf[pl.ds(c0, LANES), tok:tok + NA_KEYS]
            else:
                kw = kshift[u, :, tok - GRID_W:tok - GRID_W + NA_KEYS]
            b0 = (dy0 + 1) * GRID_W
            if b0 % LANES == 0:
                bias = bias_ref[hp, :, b0:b0 + NA_KEYS]
            else:
                wide = bias_ref[hp, :, b0 - GRID_W:b0 - GRID_W + NA_KEYS + LANES]
                bias = pltpu.roll(wide, NA_KEYS + LANES - GRID_W, 1)[:, 0:NA_KEYS]
            s = _dot(qs, kw) + bias
            s_scr[u * NA_TILE_ROWS + qi] = s
            return jnp.max(s, axis=-1, keepdims=True)

        def finish(hp, u, qi, row_max):
            c0 = pl.multiple_of(hp * LANES, LANES)
            tok, _ = window(qi)
            rows = slice(qi * GRID_W, (qi + 1) * GRID_W)
            e = jnp.exp(s_scr[u * NA_TILE_ROWS + qi] - row_max)
            vw = vbuf[tok:tok + NA_KEYS, pl.ds(2 * c0, 2 * LANES)]
            od = _dot(e.astype(_BF16), vw)
            o = od[:, 0:LANES] / od[:, LANES:]
            o_ref[0, rows, pl.ds(c0, LANES)] = jnp.where(
                low_half, o[0:GRID_W], o[GRID_W:]).astype(o_ref.dtype)

        def hp_body(j, carry):
            work = [(j * NA_HP_UNROLL + u, u, qi)
                    for u in range(NA_HP_UNROLL) for qi in range(NA_TILE_ROWS)]
            maxes = {}
            for k in range(len(work) + NA_ROW_LAG):
                if k < len(work):
                    hp, u, qi = work[k]
                    if qi == 0:
                        shift_keys(hp, u)
                    maxes[k] = scores(hp, u, qi)
                if k >= NA_ROW_LAG:
                    hp, u, qi = work[k - NA_ROW_LAG]
                    finish(hp, u, qi, maxes.pop(k - NA_ROW_LAG))
            return carry

        lax.fori_loop(0, NA_HEAD_PAIRS // NA_HP_UNROLL, hp_body, 0)

    big = NA_TILE_ROWS

    @pl.when(i == 0)
    def _():
        tile_variant(0, big)

    @pl.when(i == last)
    def _():
        tile_variant(-big, 0)

    @pl.when(jnp.logical_and(i > 0, i < last))
    def _():
        tile_variant(-big, big)


def _na(q, kt, v, bias):
    bsz, seq, _ = q.shape
    nt = seq // NA_TILE
    assert nt >= 2 and seq % NA_TILE == 0
    per = NA_TILE // NA_HALO
    nh = seq // NA_HALO
    prev_idx = lambda i: jnp.maximum(i * per - 1, 0)
    next_idx = lambda i: jnp.minimum((i + 1) * per, nh - 1)
    win_tokens = NA_TILE + 2 * NA_HALO
    return pl.pallas_call(
        functools.partial(_na_kernel, n_tiles=nt),
        grid=(bsz, nt),
        in_specs=[
            pl.BlockSpec((1, NA_TILE, MIX_WIDTH), lambda b, i: (b, i, 0)),
            pl.BlockSpec((1, MIX_WIDTH, NA_HALO), lambda b, i: (b, 0, prev_idx(i))),
            pl.BlockSpec((1, MIX_WIDTH, NA_TILE), lambda b, i: (b, 0, i)),
            pl.BlockSpec((1, MIX_WIDTH, NA_HALO), lambda b, i: (b, 0, next_idx(i))),
            pl.BlockSpec((1, NA_HALO, MIX_WIDTH), lambda b, i: (b, prev_idx(i), 0)),
            pl.BlockSpec((1, NA_TILE, MIX_WIDTH), lambda b, i: (b, i, 0)),
            pl.BlockSpec((1, NA_HALO, MIX_WIDTH), lambda b, i: (b, next_idx(i), 0)),
            _const_spec(bias.shape),
        ],
        out_specs=pl.BlockSpec((1, NA_TILE, MIX_WIDTH), lambda b, i: (b, i, 0)),
        out_shape=jax.ShapeDtypeStruct((bsz, seq, MIX_WIDTH), _BF16),
        scratch_shapes=[
            pltpu.VMEM((MIX_WIDTH, win_tokens), _BF16),
            pltpu.VMEM((NA_HP_UNROLL, LANES, win_tokens), _BF16),
            pltpu.VMEM((win_tokens, 2 * MIX_WIDTH), _BF16),
            pltpu.VMEM((NA_HP_UNROLL * NA_TILE_ROWS, 2 * GRID_W, NA_KEYS), _F32),
        ],
        compiler_params=pltpu.CompilerParams(
            dimension_semantics=("arbitrary", "arbitrary"), vmem_limit_bytes=VMEM_LIMIT),
        name="na_attention",
    )(q, kt, kt, kt, v, v, v, bias)


def _na_tail_kernel(x_ref, mix_ref, nw_ref, wqm_ref, wg0_ref, wg1_ref, wg2_ref, wgm_ref, wout_ref,
                    mkv_ref, fw_ref, o_ref, y_scr):
    x = x_ref[0]
    h = _rms(x, nw_ref[...]).astype(_BF16)
    for j, wg_ref in enumerate((wg0_ref, wg1_ref, wg2_ref)):
        c0 = j * CHUNK
        g = _dot(h, wg_ref[0])
        y_scr[:, c0:c0 + CHUNK] = (mix_ref[0, :, c0:c0 + CHUNK].astype(_F32) * _silu(g)).astype(_BF16)
    _memory_attention_into(y_scr, h, wqm_ref[0], wgm_ref[0], mkv_ref[0, 0])
    o_ref[0] = _rms(x + _dot(y_scr[...], wout_ref[0]), fw_ref[...])


def _na_tail(x, mix, mkv, layer, b_off, nw, win, wout, fw):
    bsz, seq, _ = x.shape
    tm = TOKEN_TILE
    assert CHUNK == XATTN_WIDTH and OFF_QMEM % CHUNK == 0
    wblock = (1, D_MODEL, CHUNK)
    first = OFF_QMEM // CHUNK
    return pl.pallas_call(
        _na_tail_kernel,
        grid=(bsz, seq // tm),
        in_specs=[
            pl.BlockSpec((1, tm, D_MODEL), lambda b, i: (b, i, 0)),
            pl.BlockSpec((1, tm, MIX_WIDTH), lambda b, i: (b, i, 0)),
            _const_spec((1, D_MODEL)),
        ] + [_fixed_spec(wblock, (layer, 0, first + n)) for n in range(5)] + [
            _fixed_spec((1,) + wout.shape[1:], (layer, 0, 0)),
            pl.BlockSpec((1, 1, N_MEM, 2 * XATTN_WIDTH), lambda b, i: (layer, b + b_off, 0, 0)),
            _const_spec((1, D_MODEL)),
        ],
        out_specs=pl.BlockSpec((1, tm, D_MODEL), lambda b, i: (b, i, 0)),
        out_shape=jax.ShapeDtypeStruct(x.shape, _F32),
        scratch_shapes=[pltpu.VMEM((tm, BRANCH_WIDTH), _BF16)],
        compiler_params=pltpu.CompilerParams(
            dimension_semantics=("arbitrary", "arbitrary"), vmem_limit_bytes=VMEM_LIMIT),
        name="na_tail",
    )(x, mix, nw, win, win, win, win, win, wout, mkv, fw)


def kernel(x_prompt, x_sample, mem_prompt, mem_sample, norm_w, w_in, w_out, mem_norm_w, w_mem_kv,
           conv_w, conv_b, na_rpb, final_norm_w):
    assert w_in.shape[0] == 2 and conv_w.shape[0] == 1 and na_rpb.shape[0] == 1
    win = w_in.astype(_BF16)
    wout = w_out.astype(_BF16)
    mem_all = jnp.concatenate([mem_prompt, mem_sample], axis=0)
    mkv = _mem_kv(mem_all, mem_norm_w, w_mem_kv.astype(_BF16))
    nw0 = norm_w[0].reshape(1, D_MODEL)
    nw1 = norm_w[1].reshape(1, D_MODEL)
    fw = final_norm_w.reshape(1, D_MODEL)
    cw = conv_w[0]
    cb = conv_b[0].reshape(1, MIX_WIDTH)
    wkt = win[1, :, OFF_P1:OFF_P2].T
    bias = _na_bias_table(na_rpb[0])

    def trunk(x, b_off):
        x1 = _conv_layer(x, mkv, 0, b_off, nw0, win, wout, cw, cb)
        q, kt, v = _qkv(x1, 1, nw1, win, wkt)
        mix = _na(q, kt, v, bias)
        return _na_tail(x1, mix, mkv, 1, b_off, nw1, win, wout, fw)

    return (trunk(x_prompt, 0), trunk(x_sample, mem_prompt.shape[0]))
```

```python
import functools

import numpy as np
import jax
import jax.numpy as jnp
from jax import lax
from jax.experimental import pallas as pl
from jax.experimental.pallas import tpu as pltpu

D_MODEL = 1024
GRID_W = 64
N_MEM = 256
MIX_WIDTH = 1536
XATTN_WIDTH = 512
BRANCH_WIDTH = MIX_WIDTH + XATTN_WIDTH
NA_HEAD_DIM = 64
NA_HEADS = MIX_WIDTH // NA_HEAD_DIM
NA_HEAD_PAIRS = NA_HEADS // 2
XATTN_HEADS = 4
XATTN_HEAD_DIM = XATTN_WIDTH // XATTN_HEADS
NA_WIN_H = 8
NA_WIN_W = 16
RMS_EPS = 1e-6
NEG_INF = -1e30

OFF_P1 = MIX_WIDTH
OFF_P2 = 2 * MIX_WIDTH
OFF_QMEM = 3 * MIX_WIDTH
OFF_GATE = 3 * MIX_WIDTH + XATTN_WIDTH

LANES = 128
BF16_SUBLANES = 16
TOKEN_TILE = 512
CHUNK = 512
NA_TILE_ROWS = 8
NA_TILE = NA_TILE_ROWS * GRID_W
NA_HALO = 4 * GRID_W
NA_KEYS = NA_WIN_H * GRID_W
NA_HP_UNROLL = 4
VMEM_LIMIT = 56 * 1024 * 1024

_BF16 = jnp.bfloat16
_F32 = jnp.float32


def _rms(x, w):
    return x * lax.rsqrt(jnp.mean(x * x, axis=-1, keepdims=True) + RMS_EPS) * w


def _silu(g):
    return g * (1.0 / (1.0 + jnp.exp(-g)))


def _dot(a, b):
    return jnp.dot(a, b, preferred_element_type=_F32)


def _dot_nt(a, b):
    return lax.dot_general(a, b, (((1,), (1,)), ((), ())), preferred_element_type=_F32)


def _fixed_spec(block_shape, block_index):
    return pl.BlockSpec(block_shape, lambda *_: block_index, pipeline_mode=pl.Buffered(1))


def _const_spec(shape):
    return _fixed_spec(shape, (0,) * len(shape))


def _mem_kv_kernel(mem_ref, nw_ref, w_ref, o_ref):
    hm = _rms(mem_ref[0], nw_ref[0]).astype(_BF16)
    o_ref[0, 0] = _dot(hm, w_ref[0]).astype(_BF16)


def _mem_kv(mem_all, mem_norm_w, w_mem_kv_bf16):
    depth = w_mem_kv_bf16.shape[0]
    nb = mem_all.shape[0]
    return pl.pallas_call(
        _mem_kv_kernel,
        grid=(depth, nb),
        in_specs=[
            pl.BlockSpec((1, N_MEM, D_MODEL), lambda l, b: (b, 0, 0)),
            pl.BlockSpec((1, 1, D_MODEL), lambda l, b: (l, 0, 0)),
            pl.BlockSpec((1, D_MODEL, 2 * XATTN_WIDTH), lambda l, b: (l, 0, 0)),
        ],
        out_specs=pl.BlockSpec((1, 1, N_MEM, 2 * XATTN_WIDTH), lambda l, b: (l, b, 0, 0)),
        out_shape=jax.ShapeDtypeStruct((depth, nb, N_MEM, 2 * XATTN_WIDTH), _BF16),
        name="mem_kv",
    )(mem_all, mem_norm_w.reshape(depth, 1, D_MODEL), w_mem_kv_bf16)


def _memory_attention_into(y_scr, h, wq, wg, mkv):
    qm = _dot(h, wq)
    gm = _dot(h, wg)
    scale = XATTN_HEAD_DIM ** -0.5
    for hd in range(XATTN_HEADS):
        sl = slice(hd * XATTN_HEAD_DIM, (hd + 1) * XATTN_HEAD_DIM)
        q = qm[:, sl].astype(_BF16)
        k = mkv[:, sl]
        v = mkv[:, XATTN_WIDTH + hd * XATTN_HEAD_DIM: XATTN_WIDTH + (hd + 1) * XATTN_HEAD_DIM]
        s = _dot_nt(q, k) * scale
        e = jnp.exp(s - jnp.max(s, axis=-1, keepdims=True))
        o = _dot(e.astype(_BF16), v) / jnp.sum(e, axis=-1, keepdims=True)
        y_scr[:, MIX_WIDTH + hd * XATTN_HEAD_DIM: MIX_WIDTH + (hd + 1) * XATTN_HEAD_DIM] = (
            o * _silu(gm[:, sl])).astype(_BF16)


def _conv_layer_kernel(x_ref, xp_ref, xn_ref, nw_ref, win_ref, wout_ref, mkv_ref, cw_ref, cb_ref,
                       o_ref, y_scr, *, n_tiles):
    tm = x_ref.shape[1]
    halo = xp_ref.shape[1]
    i = pl.program_id(1)
    last = n_tiles - 1
    x = x_ref[0]
    nw = nw_ref[...]
    h = _rms(x, nw).astype(_BF16)
    hp = jnp.where(i == 0, 0.0, _rms(xp_ref[0], nw)).astype(_BF16)
    hn = jnp.where(i == last, 0.0, _rms(xn_ref[0], nw)).astype(_BF16)
    h_ext = jnp.concatenate([hp, h, hn], axis=0)
    ext = tm + 2 * halo
    win_ref = win_ref.at[0]
    wout_ref = wout_ref.at[0]
    for j in range(MIX_WIDTH // CHUNK):
        c0 = j * CHUNK
        c = _dot(h_ext, win_ref[:, OFF_P1 + c0: OFF_P1 + c0 + CHUNK])
        u = _dot(h_ext, win_ref[:, OFF_P2 + c0: OFF_P2 + c0 + CHUNK])
        v = c * u
        v_prev = pltpu.roll(v, 1, 0)[halo:halo + tm]
        v_next = pltpu.roll(v, ext - 1, 0)[halo:halo + tm]
        v_cur = v[halo:halo + tm]
        cw = cw_ref[:, c0:c0 + CHUNK]
        conv = v_prev * cw[0:1] + v_cur * cw[1:2] + v_next * cw[2:3] + cb_ref[:, c0:c0 + CHUNK]
        bg = _dot(h, win_ref[:, c0:c0 + CHUNK])
        g = _dot(h, win_ref[:, OFF_GATE + c0: OFF_GATE + c0 + CHUNK])
        y_scr[:, c0:c0 + CHUNK] = (bg * conv * _silu(g)).astype(_BF16)
    _memory_attention_into(y_scr, h, win_ref[:, OFF_QMEM:OFF_QMEM + XATTN_WIDTH],
                           win_ref[:, OFF_GATE + MIX_WIDTH:], mkv_ref[0, 0])
    o_ref[0] = x + _dot(y_scr[...], wout_ref[...])


def _conv_layer(x, mkv, layer, b_off, nw, win, wout, cw, cb):
    bsz, seq, _ = x.shape
    tm = TOKEN_TILE
    halo = BF16_SUBLANES
    nt = seq // tm
    per = tm // halo
    return pl.pallas_call(
        functools.partial(_conv_layer_kernel, n_tiles=nt),
        grid=(bsz, nt),
        in_specs=[
            pl.BlockSpec((1, tm, D_MODEL), lambda b, i: (b, i, 0)),
            pl.BlockSpec((1, halo, D_MODEL), lambda b, i: (b, jnp.maximum(i * per - 1, 0), 0)),
            pl.BlockSpec((1, halo, D_MODEL), lambda b, i: (b, jnp.minimum((i + 1) * per, nt * per - 1), 0)),
            _const_spec((1, D_MODEL)),
            _fixed_spec((1,) + win.shape[1:], (layer, 0, 0)),
            _fixed_spec((1,) + wout.shape[1:], (layer, 0, 0)),
            pl.BlockSpec((1, 1, N_MEM, 2 * XATTN_WIDTH), lambda b, i: (layer, b + b_off, 0, 0)),
            _const_spec(cw.shape),
            _const_spec(cb.shape),
        ],
        out_specs=pl.BlockSpec((1, tm, D_MODEL), lambda b, i: (b, i, 0)),
        out_shape=jax.ShapeDtypeStruct(x.shape, _F32),
        scratch_shapes=[pltpu.VMEM((tm, BRANCH_WIDTH), _BF16)],
        compiler_params=pltpu.CompilerParams(
            dimension_semantics=("arbitrary", "arbitrary"), vmem_limit_bytes=VMEM_LIMIT),
        name="conv_layer",
    )(x, x, x, nw, win, wout, mkv, cw, cb)


def _qkv_kernel(x_ref, nw_ref, wq_ref, wkt_ref, wv_ref, q_ref, kt_ref, v_ref):
    h = _rms(x_ref[0], nw_ref[...]).astype(_BF16)
    q_ref[0] = (_dot(h, wq_ref[0]) * (NA_HEAD_DIM ** -0.5)).astype(_BF16)
    kt_ref[0] = _dot_nt(wkt_ref[...], h).astype(_BF16)
    v_ref[0] = _dot(h, wv_ref[0]).astype(_BF16)


def _qkv(x, layer, nw, win, wkt):
    bsz, seq, _ = x.shape
    tm = TOKEN_TILE
    wblock = (1, D_MODEL, MIX_WIDTH)
    return pl.pallas_call(
        _qkv_kernel,
        grid=(bsz, seq // tm),
        in_specs=[
            pl.BlockSpec((1, tm, D_MODEL), lambda b, i: (b, i, 0)),
            _const_spec((1, D_MODEL)),
            _fixed_spec(wblock, (layer, 0, 0)),
            _const_spec(wkt.shape),
            _fixed_spec(wblock, (layer, 0, OFF_P2 // MIX_WIDTH)),
        ],
        out_specs=[
            pl.BlockSpec((1, tm, MIX_WIDTH), lambda b, i: (b, i, 0)),
            pl.BlockSpec((1, MIX_WIDTH, tm), lambda b, i: (b, 0, i)),
            pl.BlockSpec((1, tm, MIX_WIDTH), lambda b, i: (b, i, 0)),
        ],
        out_shape=[
            jax.ShapeDtypeStruct((bsz, seq, MIX_WIDTH), _BF16),
            jax.ShapeDtypeStruct((bsz, MIX_WIDTH, seq), _BF16),
            jax.ShapeDtypeStruct((bsz, seq, MIX_WIDTH), _BF16),
        ],
        compiler_params=pltpu.CompilerParams(
            dimension_semantics=("arbitrary", "arbitrary"), vmem_limit_bytes=VMEM_LIMIT),
        name="qkv_proj",
    )(x, nw, win, wkt, win)


def _na_bias_table(rpb):
    c = np.arange(GRID_W)[:, None]
    kc = np.arange(GRID_W)[None, :]
    cstart = np.clip(c - NA_WIN_W // 2, 0, GRID_W - NA_WIN_W)
    valid = (kc >= cstart) & (kc < cstart + NA_WIN_W)
    dx = kc - c + NA_WIN_W - 1
    select = ((np.arange(2 * NA_WIN_W - 1)[:, None, None] == dx[None]) & valid[None]).astype(np.float32)
    mask = np.full((GRID_W, 2 * NA_WIN_H, GRID_W), NEG_INF, np.float32)
    mask[:, 1:, :] = np.where(valid, 0.0, NEG_INF)[:, None, :]
    rpb_padded = jnp.pad(rpb.astype(_F32), ((0, 0), (1, 0), (0, 0)))
    tab = jnp.einsum("hyd,dck->hcyk", rpb_padded, jnp.asarray(select),
                     precision=lax.Precision.HIGHEST) + jnp.asarray(mask)[None]
    return tab.reshape(NA_HEAD_PAIRS, 2 * GRID_W, 2 * NA_WIN_H * GRID_W)


def _na_kernel(q_ref, kp_ref, km_ref, kn_ref, vp_ref, vm_ref, vn_ref, bias_ref, o_ref,
               kbuf, kshift, vbuf, s_scr, *, n_tiles):
    i = pl.program_id(1)
    last = n_tiles - 1
    kbuf[:, 0:NA_HALO] = kp_ref[0]
    kbuf[:, NA_HALO:NA_HALO + NA_TILE] = km_ref[0]
    kbuf[:, NA_HALO + NA_TILE:] = kn_ref[0]
    win_tokens = NA_TILE + 2 * NA_HALO

    @pl.when(jnp.logical_and(pl.program_id(0) == 0, i == 0))
    def _():
        vbuf[...] = jnp.ones(vbuf.shape, vbuf.dtype)

    for hp in range(NA_HEAD_PAIRS):
        src = slice(hp * LANES, (hp + 1) * LANES)
        vbuf[hp, 0:NA_HALO, 0:LANES] = vp_ref[0, :, src]
        vbuf[hp, NA_HALO:NA_HALO + NA_TILE, 0:LANES] = vm_ref[0, :, src]
        vbuf[hp, NA_HALO + NA_TILE:, 0:LANES] = vn_ref[0, :, src]

    lane = lax.broadcasted_iota(jnp.int32, (GRID_W, LANES), 1)
    low_half = lane < NA_HEAD_DIM

    def tile_variant(lo, hi):
        def window(qi):
            r0 = min(max(qi - NA_WIN_H // 2, lo), hi)
            return NA_HALO + r0 * GRID_W, NA_WIN_H - 1 - qi + r0

        def shift_keys(hp, u):
            c0 = pl.multiple_of(hp * LANES, LANES)
            kw32 = pltpu.bitcast(kbuf[pl.ds(c0, LANES), :], jnp.uint32)
            kshift[u] = pltpu.bitcast(pltpu.roll(kw32, win_tokens - GRID_W, 1), _BF16)

        def scores(hp, u, qi):
            c0 = pl.multiple_of(hp * LANES, LANES)
            tok, dy0 = window(qi)
            rows = slice(qi * GRID_W, (qi + 1) * GRID_W)
            q2 = q_ref[0, rows, pl.ds(c0, LANES)]
            zero = jnp.zeros_like(q2)
            qs = jnp.concatenate([jnp.where(low_half, q2, zero),
                                  jnp.where(low_half, zero, q2)], axis=0)
            if tok % LANES == 0:
                kw = kbuf[pl.ds(c0, LANES), tok:tok + NA_KEYS]
            else:
                kw = kshift[u, :, tok - GRID_W:tok - GRID_W + NA_KEYS]
            b0 = (dy0 + 1) * GRID_W
            if b0 % LANES == 0:
                bias = bias_ref[hp, :, b0:b0 + NA_KEYS]
            else:
                wide = bias_ref[hp, :, b0 - GRID_W:b0 - GRID_W + NA_KEYS + LANES]
                bias = pltpu.roll(wide, NA_KEYS + LANES - GRID_W, 1)[:, 0:NA_KEYS]
            s = _dot(qs, kw) + bias
            s_scr[u * NA_TILE_ROWS + qi] = s
            return jnp.max(s, axis=-1, keepdims=True)

        def finish(hp, u, qi, row_max):
            c0 = pl.multiple_of(hp * LANES, LANES)
            tok, _ = window(qi)
            rows = slice(qi * GRID_W, (qi + 1) * GRID_W)
            e = jnp.exp(s_scr[u * NA_TILE_ROWS + qi] - row_max)
            vw = vbuf[hp, tok:tok + NA_KEYS, :]
            od = _dot(e.astype(_BF16), vw)
            o = od[:, 0:LANES] / od[:, LANES:]
            o_ref[0, rows, pl.ds(c0, LANES)] = jnp.where(
                low_half, o[0:GRID_W], o[GRID_W:]).astype(o_ref.dtype)

        def hp_body(j, carry):
            for u in range(NA_HP_UNROLL):
                hp = j * NA_HP_UNROLL + u
                shift_keys(hp, u)
                maxes = [scores(hp, u, qi) for qi in range(NA_TILE_ROWS)]
                for qi in range(NA_TILE_ROWS):
                    finish(hp, u, qi, maxes[qi])
            return carry

        lax.fori_loop(0, NA_HEAD_PAIRS // NA_HP_UNROLL, hp_body, 0)

    big = NA_TILE_ROWS

    @pl.when(i == 0)
    def _():
        tile_variant(0, big)

    @pl.when(i == last)
    def _():
        tile_variant(-big, 0)

    @pl.when(jnp.logical_and(i > 0, i < last))
    def _():
        tile_variant(-big, big)


def _na(q, kt, v, bias):
    bsz, seq, _ = q.shape
    nt = seq // NA_TILE
    assert nt >= 2 and seq % NA_TILE == 0
    per = NA_TILE // NA_HALO
    nh = seq // NA_HALO
    prev_idx = lambda i: jnp.maximum(i * per - 1, 0)
    next_idx = lambda i: jnp.minimum((i + 1) * per, nh - 1)
    win_tokens = NA_TILE + 2 * NA_HALO
    return pl.pallas_call(
        functools.partial(_na_kernel, n_tiles=nt),
        grid=(bsz, nt),
        in_specs=[
            pl.BlockSpec((1, NA_TILE, MIX_WIDTH), lambda b, i: (b, i, 0)),
            pl.BlockSpec((1, MIX_WIDTH, NA_HALO), lambda b, i: (b, 0, prev_idx(i))),
            pl.BlockSpec((1, MIX_WIDTH, NA_TILE), lambda b, i: (b, 0, i)),
            pl.BlockSpec((1, MIX_WIDTH, NA_HALO), lambda b, i: (b, 0, next_idx(i))),
            pl.BlockSpec((1, NA_HALO, MIX_WIDTH), lambda b, i: (b, prev_idx(i), 0)),
            pl.BlockSpec((1, NA_TILE, MIX_WIDTH), lambda b, i: (b, i, 0)),
            pl.BlockSpec((1, NA_HALO, MIX_WIDTH), lambda b, i: (b, next_idx(i), 0)),
            _const_spec(bias.shape),
        ],
        out_specs=pl.BlockSpec((1, NA_TILE, MIX_WIDTH), lambda b, i: (b, i, 0)),
        out_shape=jax.ShapeDtypeStruct((bsz, seq, MIX_WIDTH), _BF16),
        scratch_shapes=[
            pltpu.VMEM((MIX_WIDTH, win_tokens), _BF16),
            pltpu.VMEM((NA_HP_UNROLL, LANES, win_tokens), _BF16),
            pltpu.VMEM((NA_HEAD_PAIRS, win_tokens, 2 * LANES), _BF16),
            pltpu.VMEM((NA_HP_UNROLL * NA_TILE_ROWS, 2 * GRID_W, NA_KEYS), _F32),
        ],
        compiler_params=pltpu.CompilerParams(
            dimension_semantics=("arbitrary", "arbitrary"), vmem_limit_bytes=VMEM_LIMIT),
        name="na_attention",
    )(q, kt, kt, kt, v, v, v, bias)


def _na_tail_kernel(x_ref, mix_ref, nw_ref, wqm_ref, wg0_ref, wg1_ref, wg2_ref, wgm_ref, wout_ref,
                    mkv_ref, fw_ref, o_ref, y_scr):
    x = x_ref[0]
    h = _rms(x, nw_ref[...]).astype(_BF16)
    for j, wg_ref in enumerate((wg0_ref, wg1_ref, wg2_ref)):
        c0 = j * CHUNK
        g = _dot(h, wg_ref[0])
        y_scr[:, c0:c0 + CHUNK] = (mix_ref[0, :, c0:c0 + CHUNK].astype(_F32) * _silu(g)).astype(_BF16)
    _memory_attention_into(y_scr, h, wqm_ref[0], wgm_ref[0], mkv_ref[0, 0])
    o_ref[0] = _rms(x + _dot(y_scr[...], wout_ref[0]), fw_ref[...])


def _na_tail(x, mix, mkv, layer, b_off, nw, win, wout, fw):
    bsz, seq, _ = x.shape
    tm = TOKEN_TILE
    assert CHUNK == XATTN_WIDTH and OFF_QMEM % CHUNK == 0
    wblock = (1, D_MODEL, CHUNK)
    first = OFF_QMEM // CHUNK
    return pl.pallas_call(
        _na_tail_kernel,
        grid=(bsz, seq // tm),
        in_specs=[
            pl.BlockSpec((1, tm, D_MODEL), lambda b, i: (b, i, 0)),
            pl.BlockSpec((1, tm, MIX_WIDTH), lambda b, i: (b, i, 0)),
            _const_spec((1, D_MODEL)),
        ] + [_fixed_spec(wblock, (layer, 0, first + n)) for n in range(5)] + [
            _fixed_spec((1,) + wout.shape[1:], (layer, 0, 0)),
            pl.BlockSpec((1, 1, N_MEM, 2 * XATTN_WIDTH), lambda b, i: (layer, b + b_off, 0, 0)),
            _const_spec((1, D_MODEL)),
        ],
        out_specs=pl.BlockSpec((1, tm, D_MODEL), lambda b, i: (b, i, 0)),
        out_shape=jax.ShapeDtypeStruct(x.shape, _F32),
        scratch_shapes=[pltpu.VMEM((tm, BRANCH_WIDTH), _BF16)],
        compiler_params=pltpu.CompilerParams(
            dimension_semantics=("arbitrary", "arbitrary"), vmem_limit_bytes=VMEM_LIMIT),
        name="na_tail",
    )(x, mix, nw, win, win, win, win, win, wout, mkv, fw)


def kernel(x_prompt, x_sample, mem_prompt, mem_sample, norm_w, w_in, w_out, mem_norm_w, w_mem_kv,
           conv_w, conv_b, na_rpb, final_norm_w):
    assert w_in.shape[0] == 2 and conv_w.shape[0] == 1 and na_rpb.shape[0] == 1
    win = w_in.astype(_BF16)
    wout = w_out.astype(_BF16)
    mem_all = jnp.concatenate([mem_prompt, mem_sample], axis=0)
    mkv = _mem_kv(mem_all, mem_norm_w, w_mem_kv.astype(_BF16))
    nw0 = norm_w[0].reshape(1, D_MODEL)
    nw1 = norm_w[1].reshape(1, D_MODEL)
    fw = final_norm_w.reshape(1, D_MODEL)
    cw = conv_w[0]
    cb = conv_b[0].reshape(1, MIX_WIDTH)
    wkt = win[1, :, OFF_P1:OFF_P2].T
    bias = _na_bias_table(na_rpb[0])

    def trunk(x, b_off):
        x1 = _conv_layer(x, mkv, 0, b_off, nw0, win, wout, cw, cb)
        q, kt, v = _qkv(x1, 1, nw1, win, wkt)
        mix = _na(q, kt, v, bias)
        return _na_tail(x1, mix, mkv, 1, b_off, nw1, win, wout, fw)

    return (trunk(x_prompt, 0), trunk(x_sample, mem_prompt.shape[0]))
```

```python
import functools

import numpy as np
import jax
import jax.numpy as jnp
from jax import lax
from jax.experimental import pallas as pl
from jax.experimental.pallas import tpu as pltpu

D_MODEL = 1024
GRID_W = 64
N_MEM = 256
MIX_WIDTH = 1536
XATTN_WIDTH = 512
BRANCH_WIDTH = MIX_WIDTH + XATTN_WIDTH
NA_HEAD_DIM = 64
NA_HEADS = MIX_WIDTH // NA_HEAD_DIM
NA_HEAD_PAIRS = NA_HEADS // 2
XATTN_HEADS = 4
XATTN_HEAD_DIM = XATTN_WIDTH // XATTN_HEADS
NA_WIN_H = 8
NA_WIN_W = 16
RMS_EPS = 1e-6
NEG_INF = -1e30

OFF_P1 = MIX_WIDTH
OFF_P2 = 2 * MIX_WIDTH
OFF_QMEM = 3 * MIX_WIDTH
OFF_GATE = 3 * MIX_WIDTH + XATTN_WIDTH

LANES = 128
BF16_SUBLANES = 16
TOKEN_TILE = 512
WIDE_TOKEN_TILE = 1024
CHUNK = 512
NA_TILE_ROWS = 8
NA_TILE = NA_TILE_ROWS * GRID_W
NA_HALO = 4 * GRID_W
NA_KEYS = NA_WIN_H * GRID_W
NA_HP_UNROLL = 4
VMEM_LIMIT = 56 * 1024 * 1024

_BF16 = jnp.bfloat16
_F32 = jnp.float32


def _rms(x, w):
    return x * lax.rsqrt(jnp.mean(x * x, axis=-1, keepdims=True) + RMS_EPS) * w


def _silu(g):
    return g * (1.0 / (1.0 + jnp.exp(-g)))


def _dot(a, b):
    return jnp.dot(a, b, preferred_element_type=_F32)


def _dot_nt(a, b):
    return lax.dot_general(a, b, (((1,), (1,)), ((), ())), preferred_element_type=_F32)


def _fixed_spec(block_shape, block_index):
    return pl.BlockSpec(block_shape, lambda *_: block_index, pipeline_mode=pl.Buffered(1))


def _const_spec(shape):
    return _fixed_spec(shape, (0,) * len(shape))


def _mem_kv_kernel(mem_ref, nw_ref, w_ref, o_ref):
    hm = _rms(mem_ref[0], nw_ref[0]).astype(_BF16)
    o_ref[0, 0] = _dot(hm, w_ref[0]).astype(_BF16)


def _mem_kv(mem_all, mem_norm_w, w_mem_kv_bf16):
    depth = w_mem_kv_bf16.shape[0]
    nb = mem_all.shape[0]
    return pl.pallas_call(
        _mem_kv_kernel,
        grid=(depth, nb),
        in_specs=[
            pl.BlockSpec((1, N_MEM, D_MODEL), lambda l, b: (b, 0, 0)),
            pl.BlockSpec((1, 1, D_MODEL), lambda l, b: (l, 0, 0)),
            pl.BlockSpec((1, D_MODEL, 2 * XATTN_WIDTH), lambda l, b: (l, 0, 0)),
        ],
        out_specs=pl.BlockSpec((1, 1, N_MEM, 2 * XATTN_WIDTH), lambda l, b: (l, b, 0, 0)),
        out_shape=jax.ShapeDtypeStruct((depth, nb, N_MEM, 2 * XATTN_WIDTH), _BF16),
        name="mem_kv",
    )(mem_all, mem_norm_w.reshape(depth, 1, D_MODEL), w_mem_kv_bf16)


def _memory_attention_into(y_scr, h, wq, wg, mkv):
    qm = _dot(h, wq)
    gm = _dot(h, wg)
    scale = XATTN_HEAD_DIM ** -0.5
    for hd in range(XATTN_HEADS):
        sl = slice(hd * XATTN_HEAD_DIM, (hd + 1) * XATTN_HEAD_DIM)
        q = qm[:, sl].astype(_BF16)
        k = mkv[:, sl]
        v = mkv[:, XATTN_WIDTH + hd * XATTN_HEAD_DIM: XATTN_WIDTH + (hd + 1) * XATTN_HEAD_DIM]
        s = _dot_nt(q, k) * scale
        e = jnp.exp(s - jnp.max(s, axis=-1, keepdims=True))
        o = _dot(e.astype(_BF16), v) / jnp.sum(e, axis=-1, keepdims=True)
        y_scr[:, MIX_WIDTH + hd * XATTN_HEAD_DIM: MIX_WIDTH + (hd + 1) * XATTN_HEAD_DIM] = (
            o * _silu(gm[:, sl])).astype(_BF16)


def _conv_layer_kernel(x_ref, xp_ref, xn_ref, nw_ref, win_ref, wout_ref, mkv_ref, cw_ref, cb_ref,
                       o_ref, y_scr, *, n_tiles):
    tm = x_ref.shape[1]
    halo = xp_ref.shape[1]
    i = pl.program_id(1)
    last = n_tiles - 1
    x = x_ref[0]
    nw = nw_ref[...]
    h = _rms(x, nw).astype(_BF16)
    hp = jnp.where(i == 0, 0.0, _rms(xp_ref[0], nw)).astype(_BF16)
    hn = jnp.where(i == last, 0.0, _rms(xn_ref[0], nw)).astype(_BF16)
    h_ext = jnp.concatenate([hp, h, hn], axis=0)
    ext = tm + 2 * halo
    win_ref = win_ref.at[0]
    wout_ref = wout_ref.at[0]
    for j in range(MIX_WIDTH // CHUNK):
        c0 = j * CHUNK
        c = _dot(h_ext, win_ref[:, OFF_P1 + c0: OFF_P1 + c0 + CHUNK])
        u = _dot(h_ext, win_ref[:, OFF_P2 + c0: OFF_P2 + c0 + CHUNK])
        v = c * u
        v_prev = pltpu.roll(v, 1, 0)[halo:halo + tm]
        v_next = pltpu.roll(v, ext - 1, 0)[halo:halo + tm]
        v_cur = v[halo:halo + tm]
        cw = cw_ref[:, c0:c0 + CHUNK]
        conv = v_prev * cw[0:1] + v_cur * cw[1:2] + v_next * cw[2:3] + cb_ref[:, c0:c0 + CHUNK]
        bg = _dot(h, win_ref[:, c0:c0 + CHUNK])
        g = _dot(h, win_ref[:, OFF_GATE + c0: OFF_GATE + c0 + CHUNK])
        y_scr[:, c0:c0 + CHUNK] = (bg * conv * _silu(g)).astype(_BF16)
    _memory_attention_into(y_scr, h, win_ref[:, OFF_QMEM:OFF_QMEM + XATTN_WIDTH],
                           win_ref[:, OFF_GATE + MIX_WIDTH:], mkv_ref[0, 0])
    o_ref[0] = x + _dot(y_scr[...], wout_ref[...])


def _conv_layer(x, mkv, layer, b_off, nw, win, wout, cw, cb):
    bsz, seq, _ = x.shape
    tm = TOKEN_TILE
    halo = BF16_SUBLANES
    nt = seq // tm
    per = tm // halo
    return pl.pallas_call(
        functools.partial(_conv_layer_kernel, n_tiles=nt),
        grid=(bsz, nt),
        in_specs=[
            pl.BlockSpec((1, tm, D_MODEL), lambda b, i: (b, i, 0)),
            pl.BlockSpec((1, halo, D_MODEL), lambda b, i: (b, jnp.maximum(i * per - 1, 0), 0)),
            pl.BlockSpec((1, halo, D_MODEL), lambda b, i: (b, jnp.minimum((i + 1) * per, nt * per - 1), 0)),
            _const_spec((1, D_MODEL)),
            _fixed_spec((1,) + win.shape[1:], (layer, 0, 0)),
            _fixed_spec((1,) + wout.shape[1:], (layer, 0, 0)),
            pl.BlockSpec((1, 1, N_MEM, 2 * XATTN_WIDTH), lambda b, i: (layer, b + b_off, 0, 0)),
            _const_spec(cw.shape),
            _const_spec(cb.shape),
        ],
        out_specs=pl.BlockSpec((1, tm, D_MODEL), lambda b, i: (b, i, 0)),
        out_shape=jax.ShapeDtypeStruct(x.shape, _F32),
        scratch_shapes=[pltpu.VMEM((tm, BRANCH_WIDTH), _BF16)],
        compiler_params=pltpu.CompilerParams(
            dimension_semantics=("arbitrary", "arbitrary"), vmem_limit_bytes=VMEM_LIMIT),
        name="conv_layer",
    )(x, x, x, nw, win, wout, mkv, cw, cb)


def _qkv_kernel(x_ref, nw_ref, wq_ref, wkt_ref, wv_ref, q_ref, kt_ref, v_ref):
    h = _rms(x_ref[0], nw_ref[...]).astype(_BF16)
    q_ref[0] = (_dot(h, wq_ref[0]) * (NA_HEAD_DIM ** -0.5)).astype(_BF16)
    kt_ref[0] = _dot_nt(wkt_ref[...], h).astype(_BF16)
    v_ref[0] = _dot(h, wv_ref[0]).astype(_BF16)


def _qkv(x, layer, nw, win, wkt):
    bsz, seq, _ = x.shape
    tm = WIDE_TOKEN_TILE
    wblock = (1, D_MODEL, MIX_WIDTH)
    return pl.pallas_call(
        _qkv_kernel,
        grid=(bsz, seq // tm),
        in_specs=[
            pl.BlockSpec((1, tm, D_MODEL), lambda b, i: (b, i, 0)),
            _const_spec((1, D_MODEL)),
            _fixed_spec(wblock, (layer, 0, 0)),
            _const_spec(wkt.shape),
            _fixed_spec(wblock, (layer, 0, OFF_P2 // MIX_WIDTH)),
        ],
        out_specs=[
            pl.BlockSpec((1, tm, MIX_WIDTH), lambda b, i: (b, i, 0)),
            pl.BlockSpec((1, MIX_WIDTH, tm), lambda b, i: (b, 0, i)),
            pl.BlockSpec((1, tm, MIX_WIDTH), lambda b, i: (b, i, 0)),
        ],
        out_shape=[
            jax.ShapeDtypeStruct((bsz, seq, MIX_WIDTH), _BF16),
            jax.ShapeDtypeStruct((bsz, MIX_WIDTH, seq), _BF16),
            jax.ShapeDtypeStruct((bsz, seq, MIX_WIDTH), _BF16),
        ],
        compiler_params=pltpu.CompilerParams(
            dimension_semantics=("arbitrary", "arbitrary"), vmem_limit_bytes=VMEM_LIMIT),
        name="qkv_proj",
    )(x, nw, win, wkt, win)


def _na_bias_table(rpb):
    c = np.arange(GRID_W)[:, None]
    kc = np.arange(GRID_W)[None, :]
    cstart = np.clip(c - NA_WIN_W // 2, 0, GRID_W - NA_WIN_W)
    valid = (kc >= cstart) & (kc < cstart + NA_WIN_W)
    dx = kc - c + NA_WIN_W - 1
    select = ((np.arange(2 * NA_WIN_W - 1)[:, None, None] == dx[None]) & valid[None]).astype(np.float32)
    mask = np.full((GRID_W, 2 * NA_WIN_H, GRID_W), NEG_INF, np.float32)
    mask[:, 1:, :] = np.where(valid, 0.0, NEG_INF)[:, None, :]
    rpb_padded = jnp.pad(rpb.astype(_F32), ((0, 0), (1, 0), (0, 0)))
    tab = jnp.einsum("hyd,dck->hcyk", rpb_padded, jnp.asarray(select),
                     precision=lax.Precision.HIGHEST) + jnp.asarray(mask)[None]
    return tab.reshape(NA_HEAD_PAIRS, 2 * GRID_W, 2 * NA_WIN_H * GRID_W)


def _na_kernel(q_ref, kp_ref, km_ref, kn_ref, vp_ref, vm_ref, vn_ref, bias_ref, o_ref,
               kbuf, kshift, vbuf, s_scr, *, n_tiles):
    i = pl.program_id(1)
    last = n_tiles - 1
    kbuf[:, 0:NA_HALO] = kp_ref[0]
    kbuf[:, NA_HALO:NA_HALO + NA_TILE] = km_ref[0]
    kbuf[:, NA_HALO + NA_TILE:] = kn_ref[0]
    win_tokens = NA_TILE + 2 * NA_HALO

    @pl.when(jnp.logical_and(pl.program_id(0) == 0, i == 0))
    def _():
        vbuf[...] = jnp.ones(vbuf.shape, vbuf.dtype)

    for hp in range(NA_HEAD_PAIRS):
        src = slice(hp * LANES, (hp + 1) * LANES)
        vbuf[hp, 0:NA_HALO, 0:LANES] = vp_ref[0, :, src]
        vbuf[hp, NA_HALO:NA_HALO + NA_TILE, 0:LANES] = vm_ref[0, :, src]
        vbuf[hp, NA_HALO + NA_TILE:, 0:LANES] = vn_ref[0, :, src]

    lane = lax.broadcasted_iota(jnp.int32, (GRID_W, LANES), 1)
    low_half = lane < NA_HEAD_DIM

    def tile_variant(lo, hi):
        def window(qi):
            r0 = min(max(qi - NA_WIN_H // 2, lo), hi)
            return NA_HALO + r0 * GRID_W, NA_WIN_H - 1 - qi + r0

        def shift_keys(hp, u):
            c0 = pl.multiple_of(hp * LANES, LANES)
            kw32 = pltpu.bitcast(kbuf[pl.ds(c0, LANES), :], jnp.uint32)
            kshift[u] = pltpu.bitcast(pltpu.roll(kw32, win_tokens - GRID_W, 1), _BF16)

        def scores(hp, u, qi):
            c0 = pl.multiple_of(hp * LANES, LANES)
            tok, dy0 = window(qi)
            rows = slice(qi * GRID_W, (qi + 1) * GRID_W)
            q2 = q_ref[0, rows, pl.ds(c0, LANES)]
            zero = jnp.zeros_like(q2)
            qs = jnp.concatenate([jnp.where(low_half, q2, zero),
                                  jnp.where(low_half, zero, q2)], axis=0)
            if tok % LANES == 0:
                kw = kbuf[pl.ds(c0, LANES), tok:tok + NA_KEYS]
            else:
                kw = kshift[u, :, tok - GRID_W:tok - GRID_W + NA_KEYS]
            b0 = (dy0 + 1) * GRID_W
            if b0 % LANES == 0:
                bias = bias_ref[hp, :, b0:b0 + NA_KEYS]
            else:
                wide = bias_ref[hp, :, b0 - GRID_W:b0 - GRID_W + NA_KEYS + LANES]
                bias = pltpu.roll(wide, NA_KEYS + LANES - GRID_W, 1)[:, 0:NA_KEYS]
            s = _dot(qs, kw) + bias
            s_scr[u * NA_TILE_ROWS + qi] = s
            return jnp.max(s, axis=-1, keepdims=True)

        def finish(hp, u, qi, row_max):
            c0 = pl.multiple_of(hp * LANES, LANES)
            tok, _ = window(qi)
            rows = slice(qi * GRID_W, (qi + 1) * GRID_W)
            e = jnp.exp(s_scr[u * NA_TILE_ROWS + qi] - row_max)
            vw = vbuf[hp, tok:tok + NA_KEYS, :]
            od = _dot(e.astype(_BF16), vw)
            o = od[:, 0:LANES] / od[:, LANES:]
            o_ref[0, rows, pl.ds(c0, LANES)] = jnp.where(
                low_half, o[0:GRID_W], o[GRID_W:]).astype(o_ref.dtype)

        def hp_body(j, carry):
            for u in range(NA_HP_UNROLL):
                hp = j * NA_HP_UNROLL + u
                shift_keys(hp, u)
                maxes = [scores(hp, u, qi) for qi in range(NA_TILE_ROWS)]
                for qi in range(NA_TILE_ROWS):
                    finish(hp, u, qi, maxes[qi])
            return carry

        lax.fori_loop(0, NA_HEAD_PAIRS // NA_HP_UNROLL, hp_body, 0)

    big = NA_TILE_ROWS

    @pl.when(i == 0)
    def _():
        tile_variant(0, big)

    @pl.when(i == last)
    def _():
        tile_variant(-big, 0)

    @pl.when(jnp.logical_and(i > 0, i < last))
    def _():
        tile_variant(-big, big)


def _na(q, kt, v, bias):
    bsz, seq, _ = q.shape
    nt = seq // NA_TILE
    assert nt >= 2 and seq % NA_TILE == 0
    per = NA_TILE // NA_HALO
    nh = seq // NA_HALO
    prev_idx = lambda i: jnp.maximum(i * per - 1, 0)
    next_idx = lambda i: jnp.minimum((i + 1) * per, nh - 1)
    win_tokens = NA_TILE + 2 * NA_HALO
    return pl.pallas_call(
        functools.partial(_na_kernel, n_tiles=nt),
        grid=(bsz, nt),
        in_specs=[
            pl.BlockSpec((1, NA_TILE, MIX_WIDTH), lambda b, i: (b, i, 0)),
            pl.BlockSpec((1, MIX_WIDTH, NA_HALO), lambda b, i: (b, 0, prev_idx(i))),
            pl.BlockSpec((1, MIX_WIDTH, NA_TILE), lambda b, i: (b, 0, i)),
            pl.BlockSpec((1, MIX_WIDTH, NA_HALO), lambda b, i: (b, 0, next_idx(i))),
            pl.BlockSpec((1, NA_HALO, MIX_WIDTH), lambda b, i: (b, prev_idx(i), 0)),
            pl.BlockSpec((1, NA_TILE, MIX_WIDTH), lambda b, i: (b, i, 0)),
            pl.BlockSpec((1, NA_HALO, MIX_WIDTH), lambda b, i: (b, next_idx(i), 0)),
            _const_spec(bias.shape),
        ],
        out_specs=pl.BlockSpec((1, NA_TILE, MIX_WIDTH), lambda b, i: (b, i, 0)),
        out_shape=jax.ShapeDtypeStruct((bsz, seq, MIX_WIDTH), _BF16),
        scratch_shapes=[
            pltpu.VMEM((MIX_WIDTH, win_tokens), _BF16),
            pltpu.VMEM((NA_HP_UNROLL, LANES, win_tokens), _BF16),
            pltpu.VMEM((NA_HEAD_PAIRS, win_tokens, 2 * LANES), _BF16),
            pltpu.VMEM((NA_HP_UNROLL * NA_TILE_ROWS, 2 * GRID_W, NA_KEYS), _F32),
        ],
        compiler_params=pltpu.CompilerParams(
            dimension_semantics=("arbitrary", "arbitrary"), vmem_limit_bytes=VMEM_LIMIT),
        name="na_attention",
    )(q, kt, kt, kt, v, v, v, bias)


def _na_tail_kernel(x_ref, mix_ref, nw_ref, wqm_ref, wg0_ref, wg1_ref, wg2_ref, wgm_ref, wout_ref,
                    mkv_ref, fw_ref, o_ref, y_scr):
    x = x_ref[0]
    h = _rms(x, nw_ref[...]).astype(_BF16)
    for j, wg_ref in enumerate((wg0_ref, wg1_ref, wg2_ref)):
        c0 = j * CHUNK
        g = _dot(h, wg_ref[0])
        y_scr[:, c0:c0 + CHUNK] = (mix_ref[0, :, c0:c0 + CHUNK].astype(_F32) * _silu(g)).astype(_BF16)
    _memory_attention_into(y_scr, h, wqm_ref[0], wgm_ref[0], mkv_ref[0, 0])
    o_ref[0] = _rms(x + _dot(y_scr[...], wout_ref[0]), fw_ref[...])


def _na_tail(x, mix, mkv, layer, b_off, nw, win, wout, fw):
    bsz, seq, _ = x.shape
    tm = WIDE_TOKEN_TILE
    assert CHUNK == XATTN_WIDTH and OFF_QMEM % CHUNK == 0
    wblock = (1, D_MODEL, CHUNK)
    first = OFF_QMEM // CHUNK
    return pl.pallas_call(
        _na_tail_kernel,
        grid=(bsz, seq // tm),
        in_specs=[
            pl.BlockSpec((1, tm, D_MODEL), lambda b, i: (b, i, 0)),
            pl.BlockSpec((1, tm, MIX_WIDTH), lambda b, i: (b, i, 0)),
            _const_spec((1, D_MODEL)),
        ] + [_fixed_spec(wblock, (layer, 0, first + n)) for n in range(5)] + [
            _fixed_spec((1,) + wout.shape[1:], (layer, 0, 0)),
            pl.BlockSpec((1, 1, N_MEM, 2 * XATTN_WIDTH), lambda b, i: (layer, b + b_off, 0, 0)),
            _const_spec((1, D_MODEL)),
        ],
        out_specs=pl.BlockSpec((1, tm, D_MODEL), lambda b, i: (b, i, 0)),
        out_shape=jax.ShapeDtypeStruct(x.shape, _F32),
        scratch_shapes=[pltpu.VMEM((tm, BRANCH_WIDTH), _BF16)],
        compiler_params=pltpu.CompilerParams(
            dimension_semantics=("arbitrary", "arbitrary"), vmem_limit_bytes=VMEM_LIMIT),
        name="na_tail",
    )(x, mix, nw, win, win, win, win, win, wout, mkv, fw)


def kernel(x_prompt, x_sample, mem_prompt, mem_sample, norm_w, w_in, w_out, mem_norm_w, w_mem_kv,
           conv_w, conv_b, na_rpb, final_norm_w):
    assert w_in.shape[0] == 2 and conv_w.shape[0] == 1 and na_rpb.shape[0] == 1
    win = w_in.astype(_BF16)
    wout = w_out.astype(_BF16)
    mem_all = jnp.concatenate([mem_prompt, mem_sample], axis=0)
    mkv = _mem_kv(mem_all, mem_norm_w, w_mem_kv.astype(_BF16))
    nw0 = norm_w[0].reshape(1, D_MODEL)
    nw1 = norm_w[1].reshape(1, D_MODEL)
    fw = final_norm_w.reshape(1, D_MODEL)
    cw = conv_w[0]
    cb = conv_b[0].reshape(1, MIX_WIDTH)
    wkt = win[1, :, OFF_P1:OFF_P2].T
    bias = _na_bias_table(na_rpb[0])

    def trunk(x, b_off):
        x1 = _conv_layer(x, mkv, 0, b_off, nw0, win, wout, cw, cb)
        q, kt, v = _qkv(x1, 1, nw1, win, wkt)
        mix = _na(q, kt, v, bias)
        return _na_tail(x1, mix, mkv, 1, b_off, nw1, win, wout, fw)

    return (trunk(x_prompt, 0), trunk(x_sample, mem_prompt.shape[0]))
```

```python
import functools

import numpy as np
import jax
import jax.numpy as jnp
from jax import lax
from jax.experimental import pallas as pl
from jax.experimental.pallas import tpu as pltpu

D_MODEL = 1024
GRID_W = 64
N_MEM = 256
MIX_WIDTH = 1536
XATTN_WIDTH = 512
BRANCH_WIDTH = MIX_WIDTH + XATTN_WIDTH
NA_HEAD_DIM = 64
NA_HEADS = MIX_WIDTH // NA_HEAD_DIM
NA_HEAD_PAIRS = NA_HEADS // 2
XATTN_HEADS = 4
XATTN_HEAD_DIM = XATTN_WIDTH // XATTN_HEADS
NA_WIN_H = 8
NA_WIN_W = 16
RMS_EPS = 1e-6
NEG_INF = -1e30

OFF_P1 = MIX_WIDTH
OFF_P2 = 2 * MIX_WIDTH
OFF_QMEM = 3 * MIX_WIDTH
OFF_GATE = 3 * MIX_WIDTH + XATTN_WIDTH

LANES = 128
BF16_SUBLANES = 16
TOKEN_TILE = 512
WIDE_TOKEN_TILE = 1024
CONV_CHUNK = 256
CHUNK = 512
NA_TILE_ROWS = 8
NA_TILE = NA_TILE_ROWS * GRID_W
NA_HALO = 4 * GRID_W
NA_KEYS = NA_WIN_H * GRID_W
NA_HP_UNROLL = 4
VMEM_LIMIT = 56 * 1024 * 1024

_BF16 = jnp.bfloat16
_F32 = jnp.float32


def _rms(x, w):
    return x * lax.rsqrt(jnp.mean(x * x, axis=-1, keepdims=True) + RMS_EPS) * w


def _silu(g):
    return g * (1.0 / (1.0 + jnp.exp(-g)))


def _dot(a, b):
    return jnp.dot(a, b, preferred_element_type=_F32)


def _dot_nt(a, b):
    return lax.dot_general(a, b, (((1,), (1,)), ((), ())), preferred_element_type=_F32)


def _fixed_spec(block_shape, block_index):
    return pl.BlockSpec(block_shape, lambda *_: block_index, pipeline_mode=pl.Buffered(1))


def _const_spec(shape):
    return _fixed_spec(shape, (0,) * len(shape))


def _mem_kv_kernel(mem_ref, nw_ref, w_ref, o_ref):
    hm = _rms(mem_ref[0], nw_ref[0]).astype(_BF16)
    o_ref[0, 0] = _dot(hm, w_ref[0]).astype(_BF16)


def _mem_kv(mem_all, mem_norm_w, w_mem_kv_bf16):
    depth = w_mem_kv_bf16.shape[0]
    nb = mem_all.shape[0]
    return pl.pallas_call(
        _mem_kv_kernel,
        grid=(depth, nb),
        in_specs=[
            pl.BlockSpec((1, N_MEM, D_MODEL), lambda l, b: (b, 0, 0)),
            pl.BlockSpec((1, 1, D_MODEL), lambda l, b: (l, 0, 0)),
            pl.BlockSpec((1, D_MODEL, 2 * XATTN_WIDTH), lambda l, b: (l, 0, 0)),
        ],
        out_specs=pl.BlockSpec((1, 1, N_MEM, 2 * XATTN_WIDTH), lambda l, b: (l, b, 0, 0)),
        out_shape=jax.ShapeDtypeStruct((depth, nb, N_MEM, 2 * XATTN_WIDTH), _BF16),
        name="mem_kv",
    )(mem_all, mem_norm_w.reshape(depth, 1, D_MODEL), w_mem_kv_bf16)


def _memory_attention_into(y_scr, h, wq, wg, mkv):
    qm = _dot(h, wq)
    gm = _dot(h, wg)
    scale = XATTN_HEAD_DIM ** -0.5
    for hd in range(XATTN_HEADS):
        sl = slice(hd * XATTN_HEAD_DIM, (hd + 1) * XATTN_HEAD_DIM)
        q = qm[:, sl].astype(_BF16)
        k = mkv[:, sl]
        v = mkv[:, XATTN_WIDTH + hd * XATTN_HEAD_DIM: XATTN_WIDTH + (hd + 1) * XATTN_HEAD_DIM]
        s = _dot_nt(q, k) * scale
        e = jnp.exp(s - jnp.max(s, axis=-1, keepdims=True))
        o = _dot(e.astype(_BF16), v) / jnp.sum(e, axis=-1, keepdims=True)
        y_scr[:, MIX_WIDTH + hd * XATTN_HEAD_DIM: MIX_WIDTH + (hd + 1) * XATTN_HEAD_DIM] = (
            o * _silu(gm[:, sl])).astype(_BF16)


def _conv_layer_kernel(x_ref, xp_ref, xn_ref, nw_ref, win_ref, wout_ref, mkv_ref, cw_ref, cb_ref,
                       o_ref, y_scr, *, n_tiles):
    tm = x_ref.shape[1]
    halo = xp_ref.shape[1]
    i = pl.program_id(1)
    last = n_tiles - 1
    x = x_ref[0]
    nw = nw_ref[...]
    h = _rms(x, nw).astype(_BF16)
    hp = jnp.where(i == 0, 0.0, _rms(xp_ref[0], nw)).astype(_BF16)
    hn = jnp.where(i == last, 0.0, _rms(xn_ref[0], nw)).astype(_BF16)
    h_ext = jnp.concatenate([hp, h, hn], axis=0)
    ext = tm + 2 * halo
    win_ref = win_ref.at[0]
    wout_ref = wout_ref.at[0]
    for c0 in range(0, MIX_WIDTH, CONV_CHUNK):
        cols = slice(c0, c0 + CONV_CHUNK)
        c = _dot(h_ext, win_ref[:, OFF_P1 + c0: OFF_P1 + c0 + CONV_CHUNK])
        u = _dot(h_ext, win_ref[:, OFF_P2 + c0: OFF_P2 + c0 + CONV_CHUNK])
        v = c * u
        v_prev = pltpu.roll(v, 1, 0)[halo:halo + tm]
        v_next = pltpu.roll(v, ext - 1, 0)[halo:halo + tm]
        v_cur = v[halo:halo + tm]
        cw = cw_ref[:, cols]
        conv = v_prev * cw[0:1] + v_cur * cw[1:2] + v_next * cw[2:3] + cb_ref[:, cols]
        bg = _dot(h, win_ref[:, cols])
        g = _dot(h, win_ref[:, OFF_GATE + c0: OFF_GATE + c0 + CONV_CHUNK])
        y_scr[:, cols] = (bg * conv * _silu(g)).astype(_BF16)
    _memory_attention_into(y_scr, h, win_ref[:, OFF_QMEM:OFF_QMEM + XATTN_WIDTH],
                           win_ref[:, OFF_GATE + MIX_WIDTH:], mkv_ref[0, 0])
    o_ref[0] = x + _dot(y_scr[...], wout_ref[...])


def _conv_layer(x, mkv, layer, b_off, nw, win, wout, cw, cb):
    bsz, seq, _ = x.shape
    tm = TOKEN_TILE
    halo = BF16_SUBLANES
    nt = seq // tm
    per = tm // halo
    return pl.pallas_call(
        functools.partial(_conv_layer_kernel, n_tiles=nt),
        grid=(bsz, nt),
        in_specs=[
            pl.BlockSpec((1, tm, D_MODEL), lambda b, i: (b, i, 0)),
            pl.BlockSpec((1, halo, D_MODEL), lambda b, i: (b, jnp.maximum(i * per - 1, 0), 0)),
            pl.BlockSpec((1, halo, D_MODEL), lambda b, i: (b, jnp.minimum((i + 1) * per, nt * per - 1), 0)),
            _const_spec((1, D_MODEL)),
            _fixed_spec((1,) + win.shape[1:], (layer, 0, 0)),
            _fixed_spec((1,) + wout.shape[1:], (layer, 0, 0)),
            pl.BlockSpec((1, 1, N_MEM, 2 * XATTN_WIDTH), lambda b, i: (layer, b + b_off, 0, 0)),
            _const_spec(cw.shape),
            _const_spec(cb.shape),
        ],
        out_specs=pl.BlockSpec((1, tm, D_MODEL), lambda b, i: (b, i, 0)),
        out_shape=jax.ShapeDtypeStruct(x.shape, _F32),
        scratch_shapes=[pltpu.VMEM((tm, BRANCH_WIDTH), _BF16)],
        compiler_params=pltpu.CompilerParams(
            dimension_semantics=("arbitrary", "arbitrary"), vmem_limit_bytes=VMEM_LIMIT),
        name="conv_layer",
    )(x, x, x, nw, win, wout, mkv, cw, cb)


def _qkv_kernel(x_ref, nw_ref, wq_ref, wkt_ref, wv_ref, q_ref, kt_ref, v_ref):
    h = _rms(x_ref[0], nw_ref[...]).astype(_BF16)
    q_ref[0] = (_dot(h, wq_ref[0]) * (NA_HEAD_DIM ** -0.5)).astype(_BF16)
    kt_ref[0] = _dot_nt(wkt_ref[...], h).astype(_BF16)
    v_ref[0] = _dot(h, wv_ref[0]).astype(_BF16)


def _qkv(x, layer, nw, win, wkt):
    bsz, seq, _ = x.shape
    tm = WIDE_TOKEN_TILE
    wblock = (1, D_MODEL, MIX_WIDTH)
    return pl.pallas_call(
        _qkv_kernel,
        grid=(bsz, seq // tm),
        in_specs=[
            pl.BlockSpec((1, tm, D_MODEL), lambda b, i: (b, i, 0)),
            _const_spec((1, D_MODEL)),
            _fixed_spec(wblock, (layer, 0, 0)),
            _const_spec(wkt.shape),
            _fixed_spec(wblock, (layer, 0, OFF_P2 // MIX_WIDTH)),
        ],
        out_specs=[
            pl.BlockSpec((1, tm, MIX_WIDTH), lambda b, i: (b, i, 0)),
            pl.BlockSpec((1, MIX_WIDTH, tm), lambda b, i: (b, 0, i)),
            pl.BlockSpec((1, tm, MIX_WIDTH), lambda b, i: (b, i, 0)),
        ],
        out_shape=[
            jax.ShapeDtypeStruct((bsz, seq, MIX_WIDTH), _BF16),
            jax.ShapeDtypeStruct((bsz, MIX_WIDTH, seq), _BF16),
            jax.ShapeDtypeStruct((bsz, seq, MIX_WIDTH), _BF16),
        ],
        compiler_params=pltpu.CompilerParams(
            dimension_semantics=("arbitrary", "arbitrary"), vmem_limit_bytes=VMEM_LIMIT),
        name="qkv_proj",
    )(x, nw, win, wkt, win)


def _na_bias_table(rpb):
    c = np.arange(GRID_W)[:, None]
    kc = np.arange(GRID_W)[None, :]
    cstart = np.clip(c - NA_WIN_W // 2, 0, GRID_W - NA_WIN_W)
    valid = (kc >= cstart) & (kc < cstart + NA_WIN_W)
    dx = kc - c + NA_WIN_W - 1
    select = ((np.arange(2 * NA_WIN_W - 1)[:, None, None] == dx[None]) & valid[None]).astype(np.float32)
    mask = np.full((GRID_W, 2 * NA_WIN_H, GRID_W), NEG_INF, np.float32)
    mask[:, 1:, :] = np.where(valid, 0.0, NEG_INF)[:, None, :]
    rpb_padded = jnp.pad(rpb.astype(_F32), ((0, 0), (1, 0), (0, 0)))
    tab = jnp.einsum("hyd,dck->hcyk", rpb_padded, jnp.asarray(select),
                     precision=lax.Precision.HIGHEST) + jnp.asarray(mask)[None]
    return tab.reshape(NA_HEAD_PAIRS, 2 * GRID_W, 2 * NA_WIN_H * GRID_W)


def _na_kernel(q_ref, kp_ref, km_ref, kn_ref, vp_ref, vm_ref, vn_ref, bias_ref, o_ref,
               kbuf, kshift, vbuf, s_scr, *, n_tiles):
    i = pl.program_id(1)
    last = n_tiles - 1
    kbuf[:, 0:NA_HALO] = kp_ref[0]
    kbuf[:, NA_HALO:NA_HALO + NA_TILE] = km_ref[0]
    kbuf[:, NA_HALO + NA_TILE:] = kn_ref[0]
    win_tokens = NA_TILE + 2 * NA_HALO

    @pl.when(jnp.logical_and(pl.program_id(0) == 0, i == 0))
    def _():
        vbuf[...] = jnp.ones(vbuf.shape, vbuf.dtype)

    for hp in range(NA_HEAD_PAIRS):
        src = slice(hp * LANES, (hp + 1) * LANES)
        vbuf[hp, 0:NA_HALO, 0:LANES] = vp_ref[0, :, src]
        vbuf[hp, NA_HALO:NA_HALO + NA_TILE, 0:LANES] = vm_ref[0, :, src]
        vbuf[hp, NA_HALO + NA_TILE:, 0:LANES] = vn_ref[0, :, src]

    lane = lax.broadcasted_iota(jnp.int32, (GRID_W, LANES), 1)
    low_half = lane < NA_HEAD_DIM

    def tile_variant(lo, hi):
        def window(qi):
            r0 = min(max(qi - NA_WIN_H // 2, lo), hi)
            return NA_HALO + r0 * GRID_W, NA_WIN_H - 1 - qi + r0

        def shift_keys(hp, u):
            c0 = pl.multiple_of(hp * LANES, LANES)
            kw32 = pltpu.bitcast(kbuf[pl.ds(c0, LANES), :], jnp.uint32)
            kshift[u] = pltpu.bitcast(pltpu.roll(kw32, win_tokens - GRID_W, 1), _BF16)

        def scores(hp, u, qi):
            c0 = pl.multiple_of(hp * LANES, LANES)
            tok, dy0 = window(qi)
            rows = slice(qi * GRID_W, (qi + 1) * GRID_W)
            q2 = q_ref[0, rows, pl.ds(c0, LANES)]
            zero = jnp.zeros_like(q2)
            qs = jnp.concatenate([jnp.where(low_half, q2, zero),
                                  jnp.where(low_half, zero, q2)], axis=0)
            if tok % LANES == 0:
                kw = kbuf[pl.ds(c0, LANES), tok:tok + NA_KEYS]
            else:
                kw = kshift[u, :, tok - GRID_W:tok - GRID_W + NA_KEYS]
            b0 = (dy0 + 1) * GRID_W
            if b0 % LANES == 0:
                bias = bias_ref[hp, :, b0:b0 + NA_KEYS]
            else:
                wide = bias_ref[hp, :, b0 - GRID_W:b0 - GRID_W + NA_KEYS + LANES]
                bias = pltpu.roll(wide, NA_KEYS + LANES - GRID_W, 1)[:, 0:NA_KEYS]
            s = _dot(qs, kw) + bias
            s_scr[u * NA_TILE_ROWS + qi] = s
            return jnp.max(s, axis=-1, keepdims=True)

        def finish(hp, u, qi, row_max):
            c0 = pl.multiple_of(hp * LANES, LANES)
            tok, _ = window(qi)
            rows = slice(qi * GRID_W, (qi + 1) * GRID_W)
            e = jnp.exp((s_scr[u * NA_TILE_ROWS + qi] - row_max).astype(_BF16))
            vw = vbuf[hp, tok:tok + NA_KEYS, :]
            od = _dot(e, vw)
            o = od[:, 0:LANES] / od[:, LANES:]
            o_ref[0, rows, pl.ds(c0, LANES)] = jnp.where(
                low_half, o[0:GRID_W], o[GRID_W:]).astype(o_ref.dtype)

        def hp_body(j, carry):
            for u in range(NA_HP_UNROLL):
                hp = j * NA_HP_UNROLL + u
                shift_keys(hp, u)
                maxes = [scores(hp, u, qi) for qi in range(NA_TILE_ROWS)]
                for qi in range(NA_TILE_ROWS):
                    finish(hp, u, qi, maxes[qi])
            return carry

        lax.fori_loop(0, NA_HEAD_PAIRS // NA_HP_UNROLL, hp_body, 0)

    big = NA_TILE_ROWS

    @pl.when(i == 0)
    def _():
        tile_variant(0, big)

    @pl.when(i == last)
    def _():
        tile_variant(-big, 0)

    @pl.when(jnp.logical_and(i > 0, i < last))
    def _():
        tile_variant(-big, big)


def _na(q, kt, v, bias):
    bsz, seq, _ = q.shape
    nt = seq // NA_TILE
    assert nt >= 2 and seq % NA_TILE == 0
    per = NA_TILE // NA_HALO
    nh = seq // NA_HALO
    prev_idx = lambda i: jnp.maximum(i * per - 1, 0)
    next_idx = lambda i: jnp.minimum((i + 1) * per, nh - 1)
    win_tokens = NA_TILE + 2 * NA_HALO
    return pl.pallas_call(
        functools.partial(_na_kernel, n_tiles=nt),
        grid=(bsz, nt),
        in_specs=[
            pl.BlockSpec((1, NA_TILE, MIX_WIDTH), lambda b, i: (b, i, 0)),
            pl.BlockSpec((1, MIX_WIDTH, NA_HALO), lambda b, i: (b, 0, prev_idx(i))),
            pl.BlockSpec((1, MIX_WIDTH, NA_TILE), lambda b, i: (b, 0, i)),
            pl.BlockSpec((1, MIX_WIDTH, NA_HALO), lambda b, i: (b, 0, next_idx(i))),
            pl.BlockSpec((1, NA_HALO, MIX_WIDTH), lambda b, i: (b, prev_idx(i), 0)),
            pl.BlockSpec((1, NA_TILE, MIX_WIDTH), lambda b, i: (b, i, 0)),
            pl.BlockSpec((1, NA_HALO, MIX_WIDTH), lambda b, i: (b, next_idx(i), 0)),
            _const_spec(bias.shape),
        ],
        out_specs=pl.BlockSpec((1, NA_TILE, MIX_WIDTH), lambda b, i: (b, i, 0)),
        out_shape=jax.ShapeDtypeStruct((bsz, seq, MIX_WIDTH), _BF16),
        scratch_shapes=[
            pltpu.VMEM((MIX_WIDTH, win_tokens), _BF16),
            pltpu.VMEM((NA_HP_UNROLL, LANES, win_tokens), _BF16),
            pltpu.VMEM((NA_HEAD_PAIRS, win_tokens, 2 * LANES), _BF16),
            pltpu.VMEM((NA_HP_UNROLL * NA_TILE_ROWS, 2 * GRID_W, NA_KEYS), _F32),
        ],
        compiler_params=pltpu.CompilerParams(
            dimension_semantics=("arbitrary", "arbitrary"), vmem_limit_bytes=VMEM_LIMIT),
        name="na_attention",
    )(q, kt, kt, kt, v, v, v, bias)


def _na_tail_kernel(x_ref, mix_ref, nw_ref, wqm_ref, wg0_ref, wg1_ref, wg2_ref, wgm_ref, wout_ref,
                    mkv_ref, fw_ref, o_ref, y_scr):
    x = x_ref[0]
    h = _rms(x, nw_ref[...]).astype(_BF16)
    for j, wg_ref in enumerate((wg0_ref, wg1_ref, wg2_ref)):
        c0 = j * CHUNK
        g = _dot(h, wg_ref[0])
        y_scr[:, c0:c0 + CHUNK] = (mix_ref[0, :, c0:c0 + CHUNK].astype(_F32) * _silu(g)).astype(_BF16)
    _memory_attention_into(y_scr, h, wqm_ref[0], wgm_ref[0], mkv_ref[0, 0])
    o_ref[0] = _rms(x + _dot(y_scr[...], wout_ref[0]), fw_ref[...])


def _na_tail(x, mix, mkv, layer, b_off, nw, win, wout, fw):
    bsz, seq, _ = x.shape
    tm = WIDE_TOKEN_TILE
    assert CHUNK == XATTN_WIDTH and OFF_QMEM % CHUNK == 0
    wblock = (1, D_MODEL, CHUNK)
    first = OFF_QMEM // CHUNK
    return pl.pallas_call(
        _na_tail_kernel,
        grid=(bsz, seq // tm),
        in_specs=[
            pl.BlockSpec((1, tm, D_MODEL), lambda b, i: (b, i, 0)),
            pl.BlockSpec((1, tm, MIX_WIDTH), lambda b, i: (b, i, 0)),
            _const_spec((1, D_MODEL)),
        ] + [_fixed_spec(wblock, (layer, 0, first + n)) for n in range(5)] + [
            _fixed_spec((1,) + wout.shape[1:], (layer, 0, 0)),
            pl.BlockSpec((1, 1, N_MEM, 2 * XATTN_WIDTH), lambda b, i: (layer, b + b_off, 0, 0)),
            _const_spec((1, D_MODEL)),
        ],
        out_specs=pl.BlockSpec((1, tm, D_MODEL), lambda b, i: (b, i, 0)),
        out_shape=jax.ShapeDtypeStruct(x.shape, _F32),
        scratch_shapes=[pltpu.VMEM((tm, BRANCH_WIDTH), _BF16)],
        compiler_params=pltpu.CompilerParams(
            dimension_semantics=("arbitrary", "arbitrary"), vmem_limit_bytes=VMEM_LIMIT),
        name="na_tail",
    )(x, mix, nw, win, win, win, win, win, wout, mkv, fw)


def kernel(x_prompt, x_sample, mem_prompt, mem_sample, norm_w, w_in, w_out, mem_norm_w, w_mem_kv,
           conv_w, conv_b, na_rpb, final_norm_w):
    assert w_in.shape[0] == 2 and conv_w.shape[0] == 1 and na_rpb.shape[0] == 1
    win = w_in.astype(_BF16)
    wout = w_out.astype(_BF16)
    mem_all = jnp.concatenate([mem_prompt, mem_sample], axis=0)
    mkv = _mem_kv(mem_all, mem_norm_w, w_mem_kv.astype(_BF16))
    nw0 = norm_w[0].reshape(1, D_MODEL)
    nw1 = norm_w[1].reshape(1, D_MODEL)
    fw = final_norm_w.reshape(1, D_MODEL)
    cw = conv_w[0]
    cb = conv_b[0].reshape(1, MIX_WIDTH)
    wkt = win[1, :, OFF_P1:OFF_P2].T
    bias = _na_bias_table(na_rpb[0])

    def trunk(x, b_off):
        x1 = _conv_layer(x, mkv, 0, b_off, nw0, win, wout, cw, cb)
        q, kt, v = _qkv(x1, 1, nw1, win, wkt)
        mix = _na(q, kt, v, bias)
        return _na_tail(x1, mix, mkv, 1, b_off, nw1, win, wout, fw)

    return (trunk(x_prompt, 0), trunk(x_sample, mem_prompt.shape[0]))
```

```python
import functools

import numpy as np
import jax
import jax.numpy as jnp
from jax import lax
from jax.experimental import pallas as pl
from jax.experimental.pallas import tpu as pltpu

D_MODEL = 1024
GRID_W = 64
N_MEM = 256
MIX_WIDTH = 1536
XATTN_WIDTH = 512
BRANCH_WIDTH = MIX_WIDTH + XATTN_WIDTH
NA_HEAD_DIM = 64
NA_HEADS = MIX_WIDTH // NA_HEAD_DIM
NA_HEAD_PAIRS = NA_HEADS // 2
XATTN_HEADS = 4
XATTN_HEAD_DIM = XATTN_WIDTH // XATTN_HEADS
NA_WIN_H = 8
NA_WIN_W = 16
RMS_EPS = 1e-6
NEG_INF = -1e30

OFF_P1 = MIX_WIDTH
OFF_P2 = 2 * MIX_WIDTH
OFF_QMEM = 3 * MIX_WIDTH
OFF_GATE = 3 * MIX_WIDTH + XATTN_WIDTH

LANES = 128
BF16_SUBLANES = 16
TOKEN_TILE = 512
WIDE_TOKEN_TILE = 1024
CONV_CHUNK = 256
CHUNK = 512
NA_TILE_ROWS = 8
NA_TILE = NA_TILE_ROWS * GRID_W
NA_HALO = 4 * GRID_W
NA_KEYS = NA_WIN_H * GRID_W
NA_HP_UNROLL = 6
VMEM_LIMIT = 56 * 1024 * 1024

_BF16 = jnp.bfloat16
_F32 = jnp.float32


def _rms(x, w):
    return x * lax.rsqrt(jnp.mean(x * x, axis=-1, keepdims=True) + RMS_EPS) * w


def _silu(g):
    return g * (1.0 / (1.0 + jnp.exp(-g)))


def _dot(a, b):
    return jnp.dot(a, b, preferred_element_type=_F32)


def _dot_nt(a, b):
    return lax.dot_general(a, b, (((1,), (1,)), ((), ())), preferred_element_type=_F32)


def _fixed_spec(block_shape, block_index):
    return pl.BlockSpec(block_shape, lambda *_: block_index, pipeline_mode=pl.Buffered(1))


def _const_spec(shape):
    return _fixed_spec(shape, (0,) * len(shape))


def _mem_kv_kernel(mem_ref, nw_ref, w_ref, o_ref):
    hm = _rms(mem_ref[0], nw_ref[0]).astype(_BF16)
    o_ref[0, 0] = _dot(hm, w_ref[0]).astype(_BF16)


def _mem_kv(mem_all, mem_norm_w, w_mem_kv_bf16):
    depth = w_mem_kv_bf16.shape[0]
    nb = mem_all.shape[0]
    return pl.pallas_call(
        _mem_kv_kernel,
        grid=(depth, nb),
        in_specs=[
            pl.BlockSpec((1, N_MEM, D_MODEL), lambda l, b: (b, 0, 0)),
            pl.BlockSpec((1, 1, D_MODEL), lambda l, b: (l, 0, 0)),
            pl.BlockSpec((1, D_MODEL, 2 * XATTN_WIDTH), lambda l, b: (l, 0, 0)),
        ],
        out_specs=pl.BlockSpec((1, 1, N_MEM, 2 * XATTN_WIDTH), lambda l, b: (l, b, 0, 0)),
        out_shape=jax.ShapeDtypeStruct((depth, nb, N_MEM, 2 * XATTN_WIDTH), _BF16),
        name="mem_kv",
    )(mem_all, mem_norm_w.reshape(depth, 1, D_MODEL), w_mem_kv_bf16)


def _memory_attention_into(y_scr, h, wq, wg, mkv):
    qm = _dot(h, wq)
    gm = _dot(h, wg)
    scale = XATTN_HEAD_DIM ** -0.5
    for hd in range(XATTN_HEADS):
        sl = slice(hd * XATTN_HEAD_DIM, (hd + 1) * XATTN_HEAD_DIM)
        q = qm[:, sl].astype(_BF16)
        k = mkv[:, sl]
        v = mkv[:, XATTN_WIDTH + hd * XATTN_HEAD_DIM: XATTN_WIDTH + (hd + 1) * XATTN_HEAD_DIM]
        s = _dot_nt(q, k) * scale
        e = jnp.exp(s - jnp.max(s, axis=-1, keepdims=True))
        o = _dot(e.astype(_BF16), v) / jnp.sum(e, axis=-1, keepdims=True)
        y_scr[:, MIX_WIDTH + hd * XATTN_HEAD_DIM: MIX_WIDTH + (hd + 1) * XATTN_HEAD_DIM] = (
            o * _silu(gm[:, sl])).astype(_BF16)


def _conv_layer_kernel(x_ref, xp_ref, xn_ref, nw_ref, win_ref, wout_ref, mkv_ref, cw_ref, cb_ref,
                       o_ref, y_scr, *, n_tiles):
    tm = x_ref.shape[1]
    halo = xp_ref.shape[1]
    i = pl.program_id(1)
    last = n_tiles - 1
    x = x_ref[0]
    nw = nw_ref[...]
    h = _rms(x, nw).astype(_BF16)
    hp = jnp.where(i == 0, 0.0, _rms(xp_ref[0], nw)).astype(_BF16)
    hn = jnp.where(i == last, 0.0, _rms(xn_ref[0], nw)).astype(_BF16)
    h_ext = jnp.concatenate([hp, h, hn], axis=0)
    ext = tm + 2 * halo
    win_ref = win_ref.at[0]
    wout_ref = wout_ref.at[0]
    for c0 in range(0, MIX_WIDTH, CONV_CHUNK):
        cols = slice(c0, c0 + CONV_CHUNK)
        c = _dot(h_ext, win_ref[:, OFF_P1 + c0: OFF_P1 + c0 + CONV_CHUNK])
        u = _dot(h_ext, win_ref[:, OFF_P2 + c0: OFF_P2 + c0 + CONV_CHUNK])
        v = c * u
        v_prev = pltpu.roll(v, 1, 0)[halo:halo + tm]
        v_next = pltpu.roll(v, ext - 1, 0)[halo:halo + tm]
        v_cur = v[halo:halo + tm]
        cw = cw_ref[:, cols]
        conv = v_prev * cw[0:1] + v_cur * cw[1:2] + v_next * cw[2:3] + cb_ref[:, cols]
        bg = _dot(h, win_ref[:, cols])
        g = _dot(h, win_ref[:, OFF_GATE + c0: OFF_GATE + c0 + CONV_CHUNK])
        y_scr[:, cols] = (bg * conv * _silu(g)).astype(_BF16)
    _memory_attention_into(y_scr, h, win_ref[:, OFF_QMEM:OFF_QMEM + XATTN_WIDTH],
                           win_ref[:, OFF_GATE + MIX_WIDTH:], mkv_ref[0, 0])
    o_ref[0] = x + _dot(y_scr[...], wout_ref[...])


def _conv_layer(x, mkv, layer, b_off, nw, win, wout, cw, cb):
    bsz, seq, _ = x.shape
    tm = TOKEN_TILE
    halo = BF16_SUBLANES
    nt = seq // tm
    per = tm // halo
    return pl.pallas_call(
        functools.partial(_conv_layer_kernel, n_tiles=nt),
        grid=(bsz, nt),
        in_specs=[
            pl.BlockSpec((1, tm, D_MODEL), lambda b, i: (b, i, 0)),
            pl.BlockSpec((1, halo, D_MODEL), lambda b, i: (b, jnp.maximum(i * per - 1, 0), 0)),
            pl.BlockSpec((1, halo, D_MODEL), lambda b, i: (b, jnp.minimum((i + 1) * per, nt * per - 1), 0)),
            _const_spec((1, D_MODEL)),
            _fixed_spec((1,) + win.shape[1:], (layer, 0, 0)),
            _fixed_spec((1,) + wout.shape[1:], (layer, 0, 0)),
            pl.BlockSpec((1, 1, N_MEM, 2 * XATTN_WIDTH), lambda b, i: (layer, b + b_off, 0, 0)),
            _const_spec(cw.shape),
            _const_spec(cb.shape),
        ],
        out_specs=pl.BlockSpec((1, tm, D_MODEL), lambda b, i: (b, i, 0)),
        out_shape=jax.ShapeDtypeStruct(x.shape, _F32),
        scratch_shapes=[pltpu.VMEM((tm, BRANCH_WIDTH), _BF16)],
        compiler_params=pltpu.CompilerParams(
            dimension_semantics=("arbitrary", "arbitrary"), vmem_limit_bytes=VMEM_LIMIT),
        name="conv_layer",
    )(x, x, x, nw, win, wout, mkv, cw, cb)


def _qkv_kernel(x_ref, nw_ref, wq_ref, wkt_ref, wv_ref, q_ref, kt_ref, v_ref):
    h = _rms(x_ref[0], nw_ref[...]).astype(_BF16)
    q_ref[0] = (_dot(h, wq_ref[0]) * (NA_HEAD_DIM ** -0.5)).astype(_BF16)
    kt_ref[0] = _dot_nt(wkt_ref[...], h).astype(_BF16)
    v_ref[0] = _dot(h, wv_ref[0]).astype(_BF16)


def _qkv(x, layer, nw, win, wkt):
    bsz, seq, _ = x.shape
    tm = WIDE_TOKEN_TILE
    wblock = (1, D_MODEL, MIX_WIDTH)
    return pl.pallas_call(
        _qkv_kernel,
        grid=(bsz, seq // tm),
        in_specs=[
            pl.BlockSpec((1, tm, D_MODEL), lambda b, i: (b, i, 0)),
            _const_spec((1, D_MODEL)),
            _fixed_spec(wblock, (layer, 0, 0)),
            _const_spec(wkt.shape),
            _fixed_spec(wblock, (layer, 0, OFF_P2 // MIX_WIDTH)),
        ],
        out_specs=[
            pl.BlockSpec((1, tm, MIX_WIDTH), lambda b, i: (b, i, 0)),
            pl.BlockSpec((1, MIX_WIDTH, tm), lambda b, i: (b, 0, i)),
            pl.BlockSpec((1, tm, MIX_WIDTH), lambda b, i: (b, i, 0)),
        ],
        out_shape=[
            jax.ShapeDtypeStruct((bsz, seq, MIX_WIDTH), _BF16),
            jax.ShapeDtypeStruct((bsz, MIX_WIDTH, seq), _BF16),
            jax.ShapeDtypeStruct((bsz, seq, MIX_WIDTH), _BF16),
        ],
        compiler_params=pltpu.CompilerParams(
            dimension_semantics=("arbitrary", "arbitrary"), vmem_limit_bytes=VMEM_LIMIT),
        name="qkv_proj",
    )(x, nw, win, wkt, win)


def _na_bias_table(rpb):
    c = np.arange(GRID_W)[:, None]
    kc = np.arange(GRID_W)[None, :]
    cstart = np.clip(c - NA_WIN_W // 2, 0, GRID_W - NA_WIN_W)
    valid = (kc >= cstart) & (kc < cstart + NA_WIN_W)
    dx = kc - c + NA_WIN_W - 1
    select = ((np.arange(2 * NA_WIN_W - 1)[:, None, None] == dx[None]) & valid[None]).astype(np.float32)
    mask = np.full((GRID_W, 2 * NA_WIN_H, GRID_W), NEG_INF, np.float32)
    mask[:, 1:, :] = np.where(valid, 0.0, NEG_INF)[:, None, :]
    rpb_padded = jnp.pad(rpb.astype(_F32), ((0, 0), (1, 0), (0, 0)))
    tab = jnp.einsum("hyd,dck->hcyk", rpb_padded, jnp.asarray(select),
                     precision=lax.Precision.HIGHEST) + jnp.asarray(mask)[None]
    return tab.reshape(NA_HEAD_PAIRS, 2 * GRID_W, 2 * NA_WIN_H * GRID_W)


def _na_kernel(q_ref, kp_ref, km_ref, kn_ref, vp_ref, vm_ref, vn_ref, bias_ref, o_ref,
               kbuf, kshift, vbuf, s_scr, *, n_tiles):
    i = pl.program_id(1)
    last = n_tiles - 1
    kbuf[:, 0:NA_HALO] = kp_ref[0]
    kbuf[:, NA_HALO:NA_HALO + NA_TILE] = km_ref[0]
    kbuf[:, NA_HALO + NA_TILE:] = kn_ref[0]
    win_tokens = NA_TILE + 2 * NA_HALO

    @pl.when(jnp.logical_and(pl.program_id(0) == 0, i == 0))
    def _():
        vbuf[...] = jnp.ones(vbuf.shape, vbuf.dtype)

    for hp in range(NA_HEAD_PAIRS):
        src = slice(hp * LANES, (hp + 1) * LANES)
        vbuf[hp, 0:NA_HALO, 0:LANES] = vp_ref[0, :, src]
        vbuf[hp, NA_HALO:NA_HALO + NA_TILE, 0:LANES] = vm_ref[0, :, src]
        vbuf[hp, NA_HALO + NA_TILE:, 0:LANES] = vn_ref[0, :, src]

    lane = lax.broadcasted_iota(jnp.int32, (GRID_W, LANES), 1)
    low_half = lane < NA_HEAD_DIM

    def tile_variant(lo, hi):
        def window(qi):
            r0 = min(max(qi - NA_WIN_H // 2, lo), hi)
            return NA_HALO + r0 * GRID_W, NA_WIN_H - 1 - qi + r0

        def shift_keys(hp, u):
            c0 = pl.multiple_of(hp * LANES, LANES)
            kw32 = pltpu.bitcast(kbuf[pl.ds(c0, LANES), :], jnp.uint32)
            kshift[u] = pltpu.bitcast(pltpu.roll(kw32, win_tokens - GRID_W, 1), _BF16)

        def scores(hp, u, qi):
            c0 = pl.multiple_of(hp * LANES, LANES)
            tok, dy0 = window(qi)
            rows = slice(qi * GRID_W, (qi + 1) * GRID_W)
            q2 = q_ref[0, rows, pl.ds(c0, LANES)]
            zero = jnp.zeros_like(q2)
            qs = jnp.concatenate([jnp.where(low_half, q2, zero),
                                  jnp.where(low_half, zero, q2)], axis=0)
            if tok % LANES == 0:
                kw = kbuf[pl.ds(c0, LANES), tok:tok + NA_KEYS]
            else:
                kw = kshift[u, :, tok - GRID_W:tok - GRID_W + NA_KEYS]
            b0 = (dy0 + 1) * GRID_W
            if b0 % LANES == 0:
                bias = bias_ref[hp, :, b0:b0 + NA_KEYS]
            else:
                wide = bias_ref[hp, :, b0 - GRID_W:b0 - GRID_W + NA_KEYS + LANES]
                bias = pltpu.roll(wide, NA_KEYS + LANES - GRID_W, 1)[:, 0:NA_KEYS]
            s = _dot(qs, kw) + bias
            s_scr[u * NA_TILE_ROWS + qi] = s
            return jnp.max(s, axis=-1, keepdims=True)

        def finish(hp, u, qi, row_max):
            c0 = pl.multiple_of(hp * LANES, LANES)
            tok, _ = window(qi)
            rows = slice(qi * GRID_W, (qi + 1) * GRID_W)
            e = jnp.exp((s_scr[u * NA_TILE_ROWS + qi] - row_max).astype(_BF16))
            vw = vbuf[hp, tok:tok + NA_KEYS, :]
            od = _dot(e, vw)
            o = od[:, 0:LANES] / od[:, LANES:]
            o_ref[0, rows, pl.ds(c0, LANES)] = jnp.where(
                low_half, o[0:GRID_W], o[GRID_W:]).astype(o_ref.dtype)

        def hp_body(j, carry):
            for u in range(NA_HP_UNROLL):
                hp = j * NA_HP_UNROLL + u
                shift_keys(hp, u)
                maxes = [scores(hp, u, qi) for qi in range(NA_TILE_ROWS)]
                for qi in range(NA_TILE_ROWS):
                    finish(hp, u, qi, maxes[qi])
            return carry

        lax.fori_loop(0, NA_HEAD_PAIRS // NA_HP_UNROLL, hp_body, 0)

    big = NA_TILE_ROWS

    @pl.when(i == 0)
    def _():
        tile_variant(0, big)

    @pl.when(i == last)
    def _():
        tile_variant(-big, 0)

    @pl.when(jnp.logical_and(i > 0, i < last))
    def _():
        tile_variant(-big, big)


def _na(q, kt, v, bias):
    bsz, seq, _ = q.shape
    nt = seq // NA_TILE
    assert nt >= 2 and seq % NA_TILE == 0
    per = NA_TILE // NA_HALO
    nh = seq // NA_HALO
    prev_idx = lambda i: jnp.maximum(i * per - 1, 0)
    next_idx = lambda i: jnp.minimum((i + 1) * per, nh - 1)
    win_tokens = NA_TILE + 2 * NA_HALO
    return pl.pallas_call(
        functools.partial(_na_kernel, n_tiles=nt),
        grid=(bsz, nt),
        in_specs=[
            pl.BlockSpec((1, NA_TILE, MIX_WIDTH), lambda b, i: (b, i, 0)),
            pl.BlockSpec((1, MIX_WIDTH, NA_HALO), lambda b, i: (b, 0, prev_idx(i))),
            pl.BlockSpec((1, MIX_WIDTH, NA_TILE), lambda b, i: (b, 0, i)),
            pl.BlockSpec((1, MIX_WIDTH, NA_HALO), lambda b, i: (b, 0, next_idx(i))),
            pl.BlockSpec((1, NA_HALO, MIX_WIDTH), lambda b, i: (b, prev_idx(i), 0)),
            pl.BlockSpec((1, NA_TILE, MIX_WIDTH), lambda b, i: (b, i, 0)),
            pl.BlockSpec((1, NA_HALO, MIX_WIDTH), lambda b, i: (b, next_idx(i), 0)),
            _const_spec(bias.shape),
        ],
        out_specs=pl.BlockSpec((1, NA_TILE, MIX_WIDTH), lambda b, i: (b, i, 0)),
        out_shape=jax.ShapeDtypeStruct((bsz, seq, MIX_WIDTH), _BF16),
        scratch_shapes=[
            pltpu.VMEM((MIX_WIDTH, win_tokens), _BF16),
            pltpu.VMEM((NA_HP_UNROLL, LANES, win_tokens), _BF16),
            pltpu.VMEM((NA_HEAD_PAIRS, win_tokens, 2 * LANES), _BF16),
            pltpu.VMEM((NA_HP_UNROLL * NA_TILE_ROWS, 2 * GRID_W, NA_KEYS), _F32),
        ],
        compiler_params=pltpu.CompilerParams(
            dimension_semantics=("arbitrary", "arbitrary"), vmem_limit_bytes=VMEM_LIMIT),
        name="na_attention",
    )(q, kt, kt, kt, v, v, v, bias)


def _na_tail_kernel(x_ref, mix_ref, nw_ref, wqm_ref, wg0_ref, wg1_ref, wg2_ref, wgm_ref, wout_ref,
                    mkv_ref, fw_ref, o_ref, y_scr):
    half = x_ref.shape[1] // 2
    for r0 in (0, half):
        rows = slice(r0, r0 + half)
        x = x_ref[0, rows]
        h = _rms(x, nw_ref[...]).astype(_BF16)
        ys = y_scr.at[rows]
        for j, wg_ref in enumerate((wg0_ref, wg1_ref, wg2_ref)):
            c0 = j * CHUNK
            g = _dot(h, wg_ref[0])
            ys[:, c0:c0 + CHUNK] = (mix_ref[0, rows, c0:c0 + CHUNK].astype(_F32) * _silu(g)).astype(_BF16)
        _memory_attention_into(ys, h, wqm_ref[0], wgm_ref[0], mkv_ref[0, 0])
        o_ref[0, rows] = _rms(x + _dot(ys[...], wout_ref[0]), fw_ref[...])


def _na_tail(x, mix, mkv, layer, b_off, nw, win, wout, fw):
    bsz, seq, _ = x.shape
    tm = WIDE_TOKEN_TILE
    assert CHUNK == XATTN_WIDTH and OFF_QMEM % CHUNK == 0
    wblock = (1, D_MODEL, CHUNK)
    first = OFF_QMEM // CHUNK
    return pl.pallas_call(
        _na_tail_kernel,
        grid=(bsz, seq // tm),
        in_specs=[
            pl.BlockSpec((1, tm, D_MODEL), lambda b, i: (b, i, 0)),
            pl.BlockSpec((1, tm, MIX_WIDTH), lambda b, i: (b, i, 0)),
            _const_spec((1, D_MODEL)),
        ] + [_fixed_spec(wblock, (layer, 0, first + n)) for n in range(5)] + [
            _fixed_spec((1,) + wout.shape[1:], (layer, 0, 0)),
            pl.BlockSpec((1, 1, N_MEM, 2 * XATTN_WIDTH), lambda b, i: (layer, b + b_off, 0, 0)),
            _const_spec((1, D_MODEL)),
        ],
        out_specs=pl.BlockSpec((1, tm, D_MODEL), lambda b, i: (b, i, 0)),
        out_shape=jax.ShapeDtypeStruct(x.shape, _F32),
        scratch_shapes=[pltpu.VMEM((tm, BRANCH_WIDTH), _BF16)],
        compiler_params=pltpu.CompilerParams(
            dimension_semantics=("arbitrary", "arbitrary"), vmem_limit_bytes=VMEM_LIMIT),
        name="na_tail",
    )(x, mix, nw, win, win, win, win, win, wout, mkv, fw)


def kernel(x_prompt, x_sample, mem_prompt, mem_sample, norm_w, w_in, w_out, mem_norm_w, w_mem_kv,
           conv_w, conv_b, na_rpb, final_norm_w):
    assert w_in.shape[0] == 2 and conv_w.shape[0] == 1 and na_rpb.shape[0] == 1
    win = w_in.astype(_BF16)
    wout = w_out.astype(_BF16)
    mem_all = jnp.concatenate([mem_prompt, mem_sample], axis=0)
    mkv = _mem_kv(mem_all, mem_norm_w, w_mem_kv.astype(_BF16))
    nw0 = norm_w[0].reshape(1, D_MODEL)
    nw1 = norm_w[1].reshape(1, D_MODEL)
    fw = final_norm_w.reshape(1, D_MODEL)
    cw = conv_w[0]
    cb = conv_b[0].reshape(1, MIX_WIDTH)
    wkt = win[1, :, OFF_P1:OFF_P2].T
    bias = _na_bias_table(na_rpb[0])

    def trunk(x, b_off):
        x1 = _conv_layer(x, mkv, 0, b_off, nw0, win, wout, cw, cb)
        q, kt, v = _qkv(x1, 1, nw1, win, wkt)
        mix = _na(q, kt, v, bias)
        return _na_tail(x1, mix, mkv, 1, b_off, nw1, win, wout, fw)

    return (trunk(x_prompt, 0), trunk(x_sample, mem_prompt.shape[0]))
```

```python
import functools

import numpy as np
import jax
import jax.numpy as jnp
from jax import lax
from jax.experimental import pallas as pl
from jax.experimental.pallas import tpu as pltpu

D_MODEL = 1024
GRID_W = 64
N_MEM = 256
MIX_WIDTH = 1536
XATTN_WIDTH = 512
BRANCH_WIDTH = MIX_WIDTH + XATTN_WIDTH
NA_HEAD_DIM = 64
NA_HEADS = MIX_WIDTH // NA_HEAD_DIM
NA_HEAD_PAIRS = NA_HEADS // 2
XATTN_HEADS = 4
XATTN_HEAD_DIM = XATTN_WIDTH // XATTN_HEADS
NA_WIN_H = 8
NA_WIN_W = 16
RMS_EPS = 1e-6
NEG_INF = -1e30

OFF_P1 = MIX_WIDTH
OFF_P2 = 2 * MIX_WIDTH
OFF_QMEM = 3 * MIX_WIDTH
OFF_GATE = 3 * MIX_WIDTH + XATTN_WIDTH

LANES = 128
BF16_SUBLANES = 16
TOKEN_TILE = 512
WIDE_TOKEN_TILE = 1024
CONV_CHUNK = 256
CHUNK = 512
NA_TILE_ROWS = 8
NA_TILE = NA_TILE_ROWS * GRID_W
NA_HALO = 4 * GRID_W
NA_KEYS = NA_WIN_H * GRID_W
NA_HP_UNROLL = 6
VMEM_LIMIT = 56 * 1024 * 1024

_BF16 = jnp.bfloat16
_F32 = jnp.float32


def _rms(x, w):
    return x * lax.rsqrt(jnp.mean(x * x, axis=-1, keepdims=True) + RMS_EPS) * w


def _silu(g):
    return g * (1.0 / (1.0 + jnp.exp(-g)))


def _dot(a, b):
    return jnp.dot(a, b, preferred_element_type=_F32)


def _dot_nt(a, b):
    return lax.dot_general(a, b, (((1,), (1,)), ((), ())), preferred_element_type=_F32)


def _fixed_spec(block_shape, block_index):
    return pl.BlockSpec(block_shape, lambda *_: block_index, pipeline_mode=pl.Buffered(1))


def _const_spec(shape):
    return _fixed_spec(shape, (0,) * len(shape))


def _mem_kv_kernel(mem_ref, nw_ref, w_ref, o_ref):
    hm = _rms(mem_ref[0], nw_ref[0]).astype(_BF16)
    o_ref[0, 0] = _dot(hm, w_ref[0]).astype(_BF16)


def _mem_kv(mem_all, mem_norm_w, w_mem_kv_bf16):
    depth = w_mem_kv_bf16.shape[0]
    nb = mem_all.shape[0]
    return pl.pallas_call(
        _mem_kv_kernel,
        grid=(depth, nb),
        in_specs=[
            pl.BlockSpec((1, N_MEM, D_MODEL), lambda l, b: (b, 0, 0)),
            pl.BlockSpec((1, 1, D_MODEL), lambda l, b: (l, 0, 0)),
            pl.BlockSpec((1, D_MODEL, 2 * XATTN_WIDTH), lambda l, b: (l, 0, 0)),
        ],
        out_specs=pl.BlockSpec((1, 1, N_MEM, 2 * XATTN_WIDTH), lambda l, b: (l, b, 0, 0)),
        out_shape=jax.ShapeDtypeStruct((depth, nb, N_MEM, 2 * XATTN_WIDTH), _BF16),
        name="mem_kv",
    )(mem_all, mem_norm_w.reshape(depth, 1, D_MODEL), w_mem_kv_bf16)


def _memory_attention_into(y_scr, h, wq, wg, mkv):
    qm = _dot(h, wq)
    gm = _dot(h, wg)
    scale = XATTN_HEAD_DIM ** -0.5
    for hd in range(XATTN_HEADS):
        sl = slice(hd * XATTN_HEAD_DIM, (hd + 1) * XATTN_HEAD_DIM)
        q = qm[:, sl].astype(_BF16)
        k = mkv[:, sl]
        v = mkv[:, XATTN_WIDTH + hd * XATTN_HEAD_DIM: XATTN_WIDTH + (hd + 1) * XATTN_HEAD_DIM]
        s = _dot_nt(q, k) * scale
        e = jnp.exp(s - jnp.max(s, axis=-1, keepdims=True))
        o = _dot(e.astype(_BF16), v) / jnp.sum(e, axis=-1, keepdims=True)
        y_scr[:, MIX_WIDTH + hd * XATTN_HEAD_DIM: MIX_WIDTH + (hd + 1) * XATTN_HEAD_DIM] = (
            o * _silu(gm[:, sl])).astype(_BF16)


def _conv_layer_kernel(x_ref, xp_ref, xn_ref, nw_ref, win_ref, wout_ref, mkv_ref, cw_ref, cb_ref,
                       o_ref, y_scr, *, n_tiles):
    tm = x_ref.shape[1]
    halo = xp_ref.shape[1]
    i = pl.program_id(1)
    last = n_tiles - 1
    x = x_ref[0]
    nw = nw_ref[...]
    h = _rms(x, nw).astype(_BF16)
    hp = jnp.where(i == 0, 0.0, _rms(xp_ref[0], nw)).astype(_BF16)
    hn = jnp.where(i == last, 0.0, _rms(xn_ref[0], nw)).astype(_BF16)
    h_ext = jnp.concatenate([hp, h, hn], axis=0)
    ext = tm + 2 * halo
    win_ref = win_ref.at[0]
    wout_ref = wout_ref.at[0]
    for c0 in range(0, MIX_WIDTH, CONV_CHUNK):
        cols = slice(c0, c0 + CONV_CHUNK)
        c = _dot(h_ext, win_ref[:, OFF_P1 + c0: OFF_P1 + c0 + CONV_CHUNK])
        u = _dot(h_ext, win_ref[:, OFF_P2 + c0: OFF_P2 + c0 + CONV_CHUNK])
        v = c * u
        v_prev = pltpu.roll(v, 1, 0)[halo:halo + tm]
        v_next = pltpu.roll(v, ext - 1, 0)[halo:halo + tm]
        v_cur = v[halo:halo + tm]
        cw = cw_ref[:, cols]
        conv = v_prev * cw[0:1] + v_cur * cw[1:2] + v_next * cw[2:3] + cb_ref[:, cols]
        bg = _dot(h, win_ref[:, cols])
        g = _dot(h, win_ref[:, OFF_GATE + c0: OFF_GATE + c0 + CONV_CHUNK])
        y_scr[:, cols] = (bg * conv * _silu(g)).astype(_BF16)
    _memory_attention_into(y_scr, h, win_ref[:, OFF_QMEM:OFF_QMEM + XATTN_WIDTH],
                           win_ref[:, OFF_GATE + MIX_WIDTH:], mkv_ref[0, 0])
    o_ref[0] = x + _dot(y_scr[...], wout_ref[...])


def _conv_layer(x, mkv, layer, b_off, nw, win, wout, cw, cb):
    bsz, seq, _ = x.shape
    tm = TOKEN_TILE
    halo = BF16_SUBLANES
    nt = seq // tm
    per = tm // halo
    return pl.pallas_call(
        functools.partial(_conv_layer_kernel, n_tiles=nt),
        grid=(bsz, nt),
        in_specs=[
            pl.BlockSpec((1, tm, D_MODEL), lambda b, i: (b, i, 0)),
            pl.BlockSpec((1, halo, D_MODEL), lambda b, i: (b, jnp.maximum(i * per - 1, 0), 0)),
            pl.BlockSpec((1, halo, D_MODEL), lambda b, i: (b, jnp.minimum((i + 1) * per, nt * per - 1), 0)),
            _const_spec((1, D_MODEL)),
            _fixed_spec((1,) + win.shape[1:], (layer, 0, 0)),
            _fixed_spec((1,) + wout.shape[1:], (layer, 0, 0)),
            pl.BlockSpec((1, 1, N_MEM, 2 * XATTN_WIDTH), lambda b, i: (layer, b + b_off, 0, 0)),
            _const_spec(cw.shape),
            _const_spec(cb.shape),
        ],
        out_specs=pl.BlockSpec((1, tm, D_MODEL), lambda b, i: (b, i, 0)),
        out_shape=jax.ShapeDtypeStruct(x.shape, _F32),
        scratch_shapes=[pltpu.VMEM((tm, BRANCH_WIDTH), _BF16)],
        compiler_params=pltpu.CompilerParams(
            dimension_semantics=("arbitrary", "arbitrary"), vmem_limit_bytes=VMEM_LIMIT),
        name="conv_layer",
    )(x, x, x, nw, win, wout, mkv, cw, cb)


def _qkv_kernel(x_ref, nw_ref, wq_ref, wkt_ref, wv_ref, q_ref, kt_ref, v_ref):
    h = _rms(x_ref[0], nw_ref[...]).astype(_BF16)
    q_ref[0] = (_dot(h, wq_ref[0]) * (NA_HEAD_DIM ** -0.5)).astype(_BF16)
    kt_ref[0] = _dot_nt(wkt_ref[...], h).astype(_BF16)
    v_ref[0] = _dot(h, wv_ref[0]).astype(_BF16)


def _qkv(x, layer, nw, win, wkt):
    bsz, seq, _ = x.shape
    tm = WIDE_TOKEN_TILE
    wblock = (1, D_MODEL, MIX_WIDTH)
    return pl.pallas_call(
        _qkv_kernel,
        grid=(bsz, seq // tm),
        in_specs=[
            pl.BlockSpec((1, tm, D_MODEL), lambda b, i: (b, i, 0)),
            _const_spec((1, D_MODEL)),
            _fixed_spec(wblock, (layer, 0, 0)),
            _const_spec(wkt.shape),
            _fixed_spec(wblock, (layer, 0, OFF_P2 // MIX_WIDTH)),
        ],
        out_specs=[
            pl.BlockSpec((1, tm, MIX_WIDTH), lambda b, i: (b, i, 0)),
            pl.BlockSpec((1, MIX_WIDTH, tm), lambda b, i: (b, 0, i)),
            pl.BlockSpec((1, tm, MIX_WIDTH), lambda b, i: (b, i, 0)),
        ],
        out_shape=[
            jax.ShapeDtypeStruct((bsz, seq, MIX_WIDTH), _BF16),
            jax.ShapeDtypeStruct((bsz, MIX_WIDTH, seq), _BF16),
            jax.ShapeDtypeStruct((bsz, seq, MIX_WIDTH), _BF16),
        ],
        compiler_params=pltpu.CompilerParams(
            dimension_semantics=("arbitrary", "arbitrary"), vmem_limit_bytes=VMEM_LIMIT),
        name="qkv_proj",
    )(x, nw, win, wkt, win)


def _na_bias_table(rpb):
    c = np.arange(GRID_W)[:, None]
    kc = np.arange(GRID_W)[None, :]
    cstart = np.clip(c - NA_WIN_W // 2, 0, GRID_W - NA_WIN_W)
    valid = (kc >= cstart) & (kc < cstart + NA_WIN_W)
    dx = kc - c + NA_WIN_W - 1
    select = ((np.arange(2 * NA_WIN_W - 1)[:, None, None] == dx[None]) & valid[None]).astype(np.float32)
    mask = np.full((GRID_W, 2 * NA_WIN_H, GRID_W), NEG_INF, np.float32)
    mask[:, 1:, :] = np.where(valid, 0.0, NEG_INF)[:, None, :]
    rpb_padded = jnp.pad(rpb.astype(_F32), ((0, 0), (1, 0), (0, 0)))
    tab = jnp.einsum("hyd,dck->hcyk", rpb_padded, jnp.asarray(select),
                     precision=lax.Precision.HIGHEST) + jnp.asarray(mask)[None]
    return tab.reshape(NA_HEAD_PAIRS, 2 * GRID_W, 2 * NA_WIN_H * GRID_W)


def _na_kernel(q_ref, kp_ref, km_ref, kn_ref, vp_ref, vm_ref, vn_ref, bias_ref, o_ref,
               kbuf, kshift, vbuf, s_scr, *, n_tiles):
    i = pl.program_id(1)
    last = n_tiles - 1
    win_tokens = NA_TILE + 2 * NA_HALO

    @pl.when(jnp.logical_and(pl.program_id(0) == 0, i == 0))
    def _():
        vbuf[...] = jnp.ones(vbuf.shape, vbuf.dtype)

    lane = lax.broadcasted_iota(jnp.int32, (GRID_W, LANES), 1)
    low_half = lane < NA_HEAD_DIM

    def tile_variant(lo, hi):
        def window(qi):
            r0 = min(max(qi - NA_WIN_H // 2, lo), hi)
            return NA_HALO + r0 * GRID_W, NA_WIN_H - 1 - qi + r0

        def load_windows(hp, u):
            c0 = pl.multiple_of(hp * LANES, LANES)
            kbuf[u, :, 0:NA_HALO] = kp_ref[0, pl.ds(c0, LANES), :]
            kbuf[u, :, NA_HALO:NA_HALO + NA_TILE] = km_ref[0, pl.ds(c0, LANES), :]
            kbuf[u, :, NA_HALO + NA_TILE:] = kn_ref[0, pl.ds(c0, LANES), :]
            vbuf[u, 0:NA_HALO, 0:LANES] = vp_ref[0, :, pl.ds(c0, LANES)]
            vbuf[u, NA_HALO:NA_HALO + NA_TILE, 0:LANES] = vm_ref[0, :, pl.ds(c0, LANES)]
            vbuf[u, NA_HALO + NA_TILE:, 0:LANES] = vn_ref[0, :, pl.ds(c0, LANES)]
            kw32 = pltpu.bitcast(kbuf[u], jnp.uint32)
            kshift[u] = pltpu.bitcast(pltpu.roll(kw32, win_tokens - GRID_W, 1), _BF16)

        def scores(hp, u, qi):
            c0 = pl.multiple_of(hp * LANES, LANES)
            tok, dy0 = window(qi)
            rows = slice(qi * GRID_W, (qi + 1) * GRID_W)
            q2 = q_ref[0, rows, pl.ds(c0, LANES)]
            zero = jnp.zeros_like(q2)
            qs = jnp.concatenate([jnp.where(low_half, q2, zero),
                                  jnp.where(low_half, zero, q2)], axis=0)
            if tok % LANES == 0:
                kw = kbuf[u, :, tok:tok + NA_KEYS]
            else:
                kw = kshift[u, :, tok - GRID_W:tok - GRID_W + NA_KEYS]
            b0 = (dy0 + 1) * GRID_W
            if b0 % LANES == 0:
                bias = bias_ref[hp, :, b0:b0 + NA_KEYS]
            else:
                wide = bias_ref[hp, :, b0 - GRID_W:b0 - GRID_W + NA_KEYS + LANES]
                bias = pltpu.roll(wide, NA_KEYS + LANES - GRID_W, 1)[:, 0:NA_KEYS]
            s = _dot(qs, kw) + bias
            s_scr[u * NA_TILE_ROWS + qi] = s
            return jnp.max(s, axis=-1, keepdims=True)

        def finish(hp, u, qi, row_max):
            c0 = pl.multiple_of(hp * LANES, LANES)
            tok, _ = window(qi)
            rows = slice(qi * GRID_W, (qi + 1) * GRID_W)
            e = jnp.exp((s_scr[u * NA_TILE_ROWS + qi] - row_max).astype(_BF16))
            vw = vbuf[u, tok:tok + NA_KEYS, :]
            od = _dot(e, vw)
            o = od[:, 0:LANES] / od[:, LANES:]
            o_ref[0, rows, pl.ds(c0, LANES)] = jnp.where(
                low_half, o[0:GRID_W], o[GRID_W:]).astype(o_ref.dtype)

        def hp_body(j, carry):
            for u in range(NA_HP_UNROLL):
                hp = j * NA_HP_UNROLL + u
                load_windows(hp, u)
                maxes = [scores(hp, u, qi) for qi in range(NA_TILE_ROWS)]
                for qi in range(NA_TILE_ROWS):
                    finish(hp, u, qi, maxes[qi])
            return carry

        lax.fori_loop(0, NA_HEAD_PAIRS // NA_HP_UNROLL, hp_body, 0)

    big = NA_TILE_ROWS

    @pl.when(i == 0)
    def _():
        tile_variant(0, big)

    @pl.when(i == last)
    def _():
        tile_variant(-big, 0)

    @pl.when(jnp.logical_and(i > 0, i < last))
    def _():
        tile_variant(-big, big)


def _na(q, kt, v, bias):
    bsz, seq, _ = q.shape
    nt = seq // NA_TILE
    assert nt >= 2 and seq % NA_TILE == 0
    per = NA_TILE // NA_HALO
    nh = seq // NA_HALO
    prev_idx = lambda i: jnp.maximum(i * per - 1, 0)
    next_idx = lambda i: jnp.minimum((i + 1) * per, nh - 1)
    win_tokens = NA_TILE + 2 * NA_HALO
    return pl.pallas_call(
        functools.partial(_na_kernel, n_tiles=nt),
        grid=(bsz, nt),
        in_specs=[
            pl.BlockSpec((1, NA_TILE, MIX_WIDTH), lambda b, i: (b, i, 0)),
            pl.BlockSpec((1, MIX_WIDTH, NA_HALO), lambda b, i: (b, 0, prev_idx(i))),
            pl.BlockSpec((1, MIX_WIDTH, NA_TILE), lambda b, i: (b, 0, i)),
            pl.BlockSpec((1, MIX_WIDTH, NA_HALO), lambda b, i: (b, 0, next_idx(i))),
            pl.BlockSpec((1, NA_HALO, MIX_WIDTH), lambda b, i: (b, prev_idx(i), 0)),
            pl.BlockSpec((1, NA_TILE, MIX_WIDTH), lambda b, i: (b, i, 0)),
            pl.BlockSpec((1, NA_HALO, MIX_WIDTH), lambda b, i: (b, next_idx(i), 0)),
            _const_spec(bias.shape),
        ],
        out_specs=pl.BlockSpec((1, NA_TILE, MIX_WIDTH), lambda b, i: (b, i, 0)),
        out_shape=jax.ShapeDtypeStruct((bsz, seq, MIX_WIDTH), _BF16),
        scratch_shapes=[
            pltpu.VMEM((NA_HP_UNROLL, LANES, win_tokens), _BF16),
            pltpu.VMEM((NA_HP_UNROLL, LANES, win_tokens), _BF16),
            pltpu.VMEM((NA_HP_UNROLL, win_tokens, 2 * LANES), _BF16),
            pltpu.VMEM((NA_HP_UNROLL * NA_TILE_ROWS, 2 * GRID_W, NA_KEYS), _F32),
        ],
        compiler_params=pltpu.CompilerParams(
            dimension_semantics=("arbitrary", "arbitrary"), vmem_limit_bytes=VMEM_LIMIT),
        name="na_attention",
    )(q, kt, kt, kt, v, v, v, bias)


def _na_tail_kernel(x_ref, mix_ref, nw_ref, wqm_ref, wg0_ref, wg1_ref, wg2_ref, wgm_ref, wout_ref,
                    mkv_ref, fw_ref, o_ref, y_scr):
    half = x_ref.shape[1] // 2
    for r0 in (0, half):
        rows = slice(r0, r0 + half)
        x = x_ref[0, rows]
        h = _rms(x, nw_ref[...]).astype(_BF16)
        ys = y_scr.at[rows]
        for j, wg_ref in enumerate((wg0_ref, wg1_ref, wg2_ref)):
            c0 = j * CHUNK
            g = _dot(h, wg_ref[0])
            ys[:, c0:c0 + CHUNK] = (mix_ref[0, rows, c0:c0 + CHUNK].astype(_F32) * _silu(g)).astype(_BF16)
        _memory_attention_into(ys, h, wqm_ref[0], wgm_ref[0], mkv_ref[0, 0])
        o_ref[0, rows] = _rms(x + _dot(ys[...], wout_ref[0]), fw_ref[...])


def _na_tail(x, mix, mkv, layer, b_off, nw, win, wout, fw):
    bsz, seq, _ = x.shape
    tm = WIDE_TOKEN_TILE
    assert CHUNK == XATTN_WIDTH and OFF_QMEM % CHUNK == 0
    wblock = (1, D_MODEL, CHUNK)
    first = OFF_QMEM // CHUNK
    return pl.pallas_call(
        _na_tail_kernel,
        grid=(bsz, seq // tm),
        in_specs=[
            pl.BlockSpec((1, tm, D_MODEL), lambda b, i: (b, i, 0)),
            pl.BlockSpec((1, tm, MIX_WIDTH), lambda b, i: (b, i, 0)),
            _const_spec((1, D_MODEL)),
        ] + [_fixed_spec(wblock, (layer, 0, first + n)) for n in range(5)] + [
            _fixed_spec((1,) + wout.shape[1:], (layer, 0, 0)),
            pl.BlockSpec((1, 1, N_MEM, 2 * XATTN_WIDTH), lambda b, i: (layer, b + b_off, 0, 0)),
            _const_spec((1, D_MODEL)),
        ],
        out_specs=pl.BlockSpec((1, tm, D_MODEL), lambda b, i: (b, i, 0)),
        out_shape=jax.ShapeDtypeStruct(x.shape, _F32),
        scratch_shapes=[pltpu.VMEM((tm, BRANCH_WIDTH), _BF16)],
        compiler_params=pltpu.CompilerParams(
            dimension_semantics=("arbitrary", "arbitrary"), vmem_limit_bytes=VMEM_LIMIT),
        name="na_tail",
    )(x, mix, nw, win, win, win, win, win, wout, mkv, fw)


def kernel(x_prompt, x_sample, mem_prompt, mem_sample, norm_w, w_in, w_out, mem_norm_w, w_mem_kv,
           conv_w, conv_b, na_rpb, final_norm_w):
    assert w_in.shape[0] == 2 and conv_w.shape[0] == 1 and na_rpb.shape[0] == 1
    win = w_in.astype(_BF16)
    wout = w_out.astype(_BF16)
    mem_all = jnp.concatenate([mem_prompt, mem_sample], axis=0)
    mkv = _mem_kv(mem_all, mem_norm_w, w_mem_kv.astype(_BF16))
    nw0 = norm_w[0].reshape(1, D_MODEL)
    nw1 = norm_w[1].reshape(1, D_MODEL)
    fw = final_norm_w.reshape(1, D_MODEL)
    cw = conv_w[0]
    cb = conv_b[0].reshape(1, MIX_WIDTH)
    wkt = win[1, :, OFF_P1:OFF_P2].T
    bias = _na_bias_table(na_rpb[0])

    def trunk(x, b_off):
        x1 = _conv_layer(x, mkv, 0, b_off, nw0, win, wout, cw, cb)
        q, kt, v = _qkv(x1, 1, nw1, win, wkt)
        mix = _na(q, kt, v, bias)
        return _na_tail(x1, mix, mkv, 1, b_off, nw1, win, wout, fw)

    return (trunk(x_prompt, 0), trunk(x_sample, mem_prompt.shape[0]))
```

```python
import functools

import numpy as np
import jax
import jax.numpy as jnp
from jax import lax
from jax.experimental import pallas as pl
from jax.experimental.pallas import tpu as pltpu

D_MODEL = 1024
GRID_W = 64
N_MEM = 256
MIX_WIDTH = 1536
XATTN_WIDTH = 512
BRANCH_WIDTH = MIX_WIDTH + XATTN_WIDTH
NA_HEAD_DIM = 64
NA_HEADS = MIX_WIDTH // NA_HEAD_DIM
NA_HEAD_PAIRS = NA_HEADS // 2
XATTN_HEADS = 4
XATTN_HEAD_DIM = XATTN_WIDTH // XATTN_HEADS
NA_WIN_H = 8
NA_WIN_W = 16
RMS_EPS = 1e-6
NEG_INF = -1e30

OFF_P1 = MIX_WIDTH
OFF_P2 = 2 * MIX_WIDTH
OFF_QMEM = 3 * MIX_WIDTH
OFF_GATE = 3 * MIX_WIDTH + XATTN_WIDTH

LANES = 128
BF16_SUBLANES = 16
TOKEN_TILE = 1024
ROW_PASSES = 2
CONV_CHUNK = 256
CHUNK = 512
NA_TILE_ROWS = 8
NA_TILE = NA_TILE_ROWS * GRID_W
NA_HALO = 4 * GRID_W
NA_KEYS = NA_WIN_H * GRID_W
NA_HP_UNROLL = 6
VMEM_LIMIT = 56 * 1024 * 1024

_BF16 = jnp.bfloat16
_F32 = jnp.float32


def _rms(x, w):
    return x * lax.rsqrt(jnp.mean(x * x, axis=-1, keepdims=True) + RMS_EPS) * w


def _silu(g):
    return g * (1.0 / (1.0 + jnp.exp(-g)))


def _dot(a, b):
    return jnp.dot(a, b, preferred_element_type=_F32)


def _dot_nt(a, b):
    return lax.dot_general(a, b, (((1,), (1,)), ((), ())), preferred_element_type=_F32)


def _fixed_spec(block_shape, block_index):
    return pl.BlockSpec(block_shape, lambda *_: block_index, pipeline_mode=pl.Buffered(1))


def _const_spec(shape):
    return _fixed_spec(shape, (0,) * len(shape))


def _mem_kv_kernel(mem_ref, nw_ref, w_ref, o_ref):
    hm = _rms(mem_ref[...], nw_ref[0]).astype(_BF16)
    o_ref[0] = _dot(hm, w_ref[0]).astype(_BF16)


def _mem_kv(mem_all, mem_norm_w, w_mem_kv_bf16):
    depth = w_mem_kv_bf16.shape[0]
    nb = mem_all.shape[0]
    rows = nb * N_MEM
    mkv = pl.pallas_call(
        _mem_kv_kernel,
        grid=(depth,),
        in_specs=[
            pl.BlockSpec((rows, D_MODEL), lambda l: (0, 0)),
            pl.BlockSpec((1, 1, D_MODEL), lambda l: (l, 0, 0)),
            pl.BlockSpec((1, D_MODEL, 2 * XATTN_WIDTH), lambda l: (l, 0, 0)),
        ],
        out_specs=pl.BlockSpec((1, rows, 2 * XATTN_WIDTH), lambda l: (l, 0, 0)),
        out_shape=jax.ShapeDtypeStruct((depth, rows, 2 * XATTN_WIDTH), _BF16),
        compiler_params=pltpu.CompilerParams(
            dimension_semantics=("arbitrary",), vmem_limit_bytes=VMEM_LIMIT),
        name="mem_kv",
    )(mem_all.reshape(rows, D_MODEL), mem_norm_w.reshape(depth, 1, D_MODEL), w_mem_kv_bf16)
    return mkv.reshape(depth, nb, N_MEM, 2 * XATTN_WIDTH)


def _memory_attention_into(y_scr, h, wq, wg, mkv):
    qm = _dot(h, wq)
    gm = _dot(h, wg)
    scale = XATTN_HEAD_DIM ** -0.5
    for hd in range(XATTN_HEADS):
        sl = slice(hd * XATTN_HEAD_DIM, (hd + 1) * XATTN_HEAD_DIM)
        q = qm[:, sl].astype(_BF16)
        k = mkv[:, sl]
        v = mkv[:, XATTN_WIDTH + hd * XATTN_HEAD_DIM: XATTN_WIDTH + (hd + 1) * XATTN_HEAD_DIM]
        s = _dot_nt(q, k) * scale
        e = jnp.exp(s - jnp.max(s, axis=-1, keepdims=True))
        o = _dot(e.astype(_BF16), v) / jnp.sum(e, axis=-1, keepdims=True)
        y_scr[:, MIX_WIDTH + hd * XATTN_HEAD_DIM: MIX_WIDTH + (hd + 1) * XATTN_HEAD_DIM] = (
            o * _silu(gm[:, sl])).astype(_BF16)


def _conv_layer_kernel(x_ref, xp_ref, xn_ref, nw_ref, win_ref, wout_ref, mkv_ref, cw_ref, cb_ref,
                       o_ref, y_scr, *, n_tiles):
    halo = xp_ref.shape[1]
    i = pl.program_id(1)
    last = n_tiles - 1
    nw = nw_ref[...]
    win_ref = win_ref.at[0]
    wout_ref = wout_ref.at[0]
    total = x_ref.shape[1]
    tm = total // ROW_PASSES
    ext = tm + 2 * halo
    for r0 in range(0, total, tm):
        rows = slice(r0, r0 + tm)
        x = x_ref[0, rows]
        h = _rms(x, nw).astype(_BF16)
        if r0 == 0:
            hp = jnp.where(i == 0, 0.0, _rms(xp_ref[0], nw)).astype(_BF16)
        else:
            hp = _rms(x_ref[0, r0 - halo:r0], nw).astype(_BF16)
        if r0 + tm == total:
            hn = jnp.where(i == last, 0.0, _rms(xn_ref[0], nw)).astype(_BF16)
        else:
            hn = _rms(x_ref[0, r0 + tm:r0 + tm + halo], nw).astype(_BF16)
        h_ext = jnp.concatenate([hp, h, hn], axis=0)
        ys = y_scr.at[rows]
        for c0 in range(0, MIX_WIDTH, CONV_CHUNK):
            cols = slice(c0, c0 + CONV_CHUNK)
            c = _dot(h_ext, win_ref[:, OFF_P1 + c0: OFF_P1 + c0 + CONV_CHUNK])
            u = _dot(h_ext, win_ref[:, OFF_P2 + c0: OFF_P2 + c0 + CONV_CHUNK])
            v = c * u
            v_prev = pltpu.roll(v, 1, 0)[halo:halo + tm]
            v_next = pltpu.roll(v, ext - 1, 0)[halo:halo + tm]
            v_cur = v[halo:halo + tm]
            cw = cw_ref[:, cols]
            conv = v_prev * cw[0:1] + v_cur * cw[1:2] + v_next * cw[2:3] + cb_ref[:, cols]
            bg = _dot(h, win_ref[:, cols])
            g = _dot(h, win_ref[:, OFF_GATE + c0: OFF_GATE + c0 + CONV_CHUNK])
            ys[:, cols] = (bg * conv * _silu(g)).astype(_BF16)
        _memory_attention_into(ys, h, win_ref[:, OFF_QMEM:OFF_QMEM + XATTN_WIDTH],
                               win_ref[:, OFF_GATE + MIX_WIDTH:], mkv_ref[0, 0])
        o_ref[0, rows] = x + _dot(ys[...], wout_ref[...])


def _conv_layer(x, mkv, layer, b_off, nw, win, wout, cw, cb):
    bsz, seq, _ = x.shape
    tm = TOKEN_TILE
    halo = BF16_SUBLANES
    nt = seq // tm
    per = tm // halo
    return pl.pallas_call(
        functools.partial(_conv_layer_kernel, n_tiles=nt),
        grid=(bsz, nt),
        in_specs=[
            pl.BlockSpec((1, tm, D_MODEL), lambda b, i: (b, i, 0)),
            pl.BlockSpec((1, halo, D_MODEL), lambda b, i: (b, jnp.maximum(i * per - 1, 0), 0)),
            pl.BlockSpec((1, halo, D_MODEL), lambda b, i: (b, jnp.minimum((i + 1) * per, nt * per - 1), 0)),
            _const_spec((1, D_MODEL)),
            _fixed_spec((1,) + win.shape[1:], (layer, 0, 0)),
            _fixed_spec((1,) + wout.shape[1:], (layer, 0, 0)),
            pl.BlockSpec((1, 1, N_MEM, 2 * XATTN_WIDTH), lambda b, i: (layer, b + b_off, 0, 0)),
            _const_spec(cw.shape),
            _const_spec(cb.shape),
        ],
        out_specs=pl.BlockSpec((1, tm, D_MODEL), lambda b, i: (b, i, 0)),
        out_shape=jax.ShapeDtypeStruct(x.shape, _F32),
        scratch_shapes=[pltpu.VMEM((tm, BRANCH_WIDTH), _BF16)],
        compiler_params=pltpu.CompilerParams(
            dimension_semantics=("arbitrary", "arbitrary"), vmem_limit_bytes=VMEM_LIMIT),
        name="conv_layer",
    )(x, x, x, nw, win, wout, mkv, cw, cb)


def _qkv_kernel(x_ref, nw_ref, wq_ref, wkt_ref, wv_ref, q_ref, kt_ref, v_ref):
    h = _rms(x_ref[0], nw_ref[...]).astype(_BF16)
    q_ref[0] = (_dot(h, wq_ref[0]) * (NA_HEAD_DIM ** -0.5)).astype(_BF16)
    kt_ref[0] = _dot_nt(wkt_ref[...], h).astype(_BF16)
    v_ref[0] = _dot(h, wv_ref[0]).astype(_BF16)


def _qkv(x, layer, nw, win, wkt):
    bsz, seq, _ = x.shape
    tm = TOKEN_TILE
    wblock = (1, D_MODEL, MIX_WIDTH)
    return pl.pallas_call(
        _qkv_kernel,
        grid=(bsz, seq // tm),
        in_specs=[
            pl.BlockSpec((1, tm, D_MODEL), lambda b, i: (b, i, 0)),
            _const_spec((1, D_MODEL)),
            _fixed_spec(wblock, (layer, 0, 0)),
            _const_spec(wkt.shape),
            _fixed_spec(wblock, (layer, 0, OFF_P2 // MIX_WIDTH)),
        ],
        out_specs=[
            pl.BlockSpec((1, tm, MIX_WIDTH), lambda b, i: (b, i, 0)),
            pl.BlockSpec((1, MIX_WIDTH, tm), lambda b, i: (b, 0, i)),
            pl.BlockSpec((1, tm, MIX_WIDTH), lambda b, i: (b, i, 0)),
        ],
        out_shape=[
            jax.ShapeDtypeStruct((bsz, seq, MIX_WIDTH), _BF16),
            jax.ShapeDtypeStruct((bsz, MIX_WIDTH, seq), _BF16),
            jax.ShapeDtypeStruct((bsz, seq, MIX_WIDTH), _BF16),
        ],
        compiler_params=pltpu.CompilerParams(
            dimension_semantics=("arbitrary", "arbitrary"), vmem_limit_bytes=VMEM_LIMIT),
        name="qkv_proj",
    )(x, nw, win, wkt, win)


def _na_bias_table(rpb):
    c = np.arange(GRID_W)[:, None]
    kc = np.arange(GRID_W)[None, :]
    cstart = np.clip(c - NA_WIN_W // 2, 0, GRID_W - NA_WIN_W)
    valid = (kc >= cstart) & (kc < cstart + NA_WIN_W)
    dx = kc - c + NA_WIN_W - 1
    select = ((np.arange(2 * NA_WIN_W - 1)[:, None, None] == dx[None]) & valid[None]).astype(np.float32)
    mask = np.full((GRID_W, 2 * NA_WIN_H, GRID_W), NEG_INF, np.float32)
    mask[:, 1:, :] = np.where(valid, 0.0, NEG_INF)[:, None, :]
    rpb_padded = jnp.pad(rpb, ((0, 0), (1, 0), (0, 0)))
    tab = jnp.einsum("hyd,dck->hcyk", rpb_padded, jnp.asarray(select),
                     precision=lax.Precision.HIGHEST) + jnp.asarray(mask)[None]
    return tab.reshape(NA_HEAD_PAIRS, 2 * GRID_W, 2 * NA_WIN_H * GRID_W)


def _na_kernel(q_ref, kp_ref, km_ref, kn_ref, vp_ref, vm_ref, vn_ref, bias_ref, o_ref,
               kbuf, kshift, vbuf, s_scr, *, n_tiles):
    i = pl.program_id(1)
    last = n_tiles - 1
    win_tokens = NA_TILE + 2 * NA_HALO

    @pl.when(jnp.logical_and(pl.program_id(0) == 0, i == 0))
    def _():
        vbuf[...] = jnp.ones(vbuf.shape, vbuf.dtype)

    lane = lax.broadcasted_iota(jnp.int32, (GRID_W, LANES), 1)
    low_half = lane < NA_HEAD_DIM

    def tile_variant(lo, hi):
        def window(qi):
            r0 = min(max(qi - NA_WIN_H // 2, lo), hi)
            return NA_HALO + r0 * GRID_W, NA_WIN_H - 1 - qi + r0

        def load_windows(hp, u):
            c0 = pl.multiple_of(hp * LANES, LANES)
            kbuf[u, :, 0:NA_HALO] = kp_ref[0, pl.ds(c0, LANES), :]
            kbuf[u, :, NA_HALO:NA_HALO + NA_TILE] = km_ref[0, pl.ds(c0, LANES), :]
            kbuf[u, :, NA_HALO + NA_TILE:] = kn_ref[0, pl.ds(c0, LANES), :]
            vbuf[u, 0:NA_HALO, 0:LANES] = vp_ref[0, :, pl.ds(c0, LANES)]
            vbuf[u, NA_HALO:NA_HALO + NA_TILE, 0:LANES] = vm_ref[0, :, pl.ds(c0, LANES)]
            vbuf[u, NA_HALO + NA_TILE:, 0:LANES] = vn_ref[0, :, pl.ds(c0, LANES)]
            kw32 = pltpu.bitcast(kbuf[u], jnp.uint32)
            kshift[u] = pltpu.bitcast(pltpu.roll(kw32, win_tokens - GRID_W, 1), _BF16)

        def scores(hp, u, qi):
            c0 = pl.multiple_of(hp * LANES, LANES)
            tok, dy0 = window(qi)
            rows = slice(qi * GRID_W, (qi + 1) * GRID_W)
            q2 = q_ref[0, rows, pl.ds(c0, LANES)]
            zero = jnp.zeros_like(q2)
            qs = jnp.concatenate([jnp.where(low_half, q2, zero),
                                  jnp.where(low_half, zero, q2)], axis=0)
            if tok % LANES == 0:
                kw = kbuf[u, :, tok:tok + NA_KEYS]
            else:
                kw = kshift[u, :, tok - GRID_W:tok - GRID_W + NA_KEYS]
            b0 = (dy0 + 1) * GRID_W
            if b0 % LANES == 0:
                bias = bias_ref[hp, :, b0:b0 + NA_KEYS]
            else:
                wide = bias_ref[hp, :, b0 - GRID_W:b0 - GRID_W + NA_KEYS + LANES]
                bias = pltpu.roll(wide, NA_KEYS + LANES - GRID_W, 1)[:, 0:NA_KEYS]
            s = _dot(qs, kw) + bias
            s_scr[u * NA_TILE_ROWS + qi] = s
            return jnp.max(s, axis=-1, keepdims=True)

        def finish(hp, u, qi, row_max):
            c0 = pl.multiple_of(hp * LANES, LANES)
            tok, _ = window(qi)
            rows = slice(qi * GRID_W, (qi + 1) * GRID_W)
            e = jnp.exp((s_scr[u * NA_TILE_ROWS + qi] - row_max).astype(_BF16))
            vw = vbuf[u, tok:tok + NA_KEYS, :]
            od = _dot(e, vw)
            o = od[:, 0:LANES] / od[:, LANES:]
            o_ref[0, rows, pl.ds(c0, LANES)] = jnp.where(
                low_half, o[0:GRID_W], o[GRID_W:]).astype(o_ref.dtype)

        def hp_body(j, carry):
            for u in range(NA_HP_UNROLL):
                hp = j * NA_HP_UNROLL + u
                load_windows(hp, u)
                maxes = [scores(hp, u, qi) for qi in range(NA_TILE_ROWS)]
                for qi in range(NA_TILE_ROWS):
                    finish(hp, u, qi, maxes[qi])
            return carry

        lax.fori_loop(0, NA_HEAD_PAIRS // NA_HP_UNROLL, hp_body, 0)

    big = NA_TILE_ROWS

    @pl.when(i == 0)
    def _():
        tile_variant(0, big)

    @pl.when(i == last)
    def _():
        tile_variant(-big, 0)

    @pl.when(jnp.logical_and(i > 0, i < last))
    def _():
        tile_variant(-big, big)


def _na(q, kt, v, bias):
    bsz, seq, _ = q.shape
    nt = seq // NA_TILE
    assert nt >= 2 and seq % NA_TILE == 0
    per = NA_TILE // NA_HALO
    nh = seq // NA_HALO
    prev_idx = lambda i: jnp.maximum(i * per - 1, 0)
    next_idx = lambda i: jnp.minimum((i + 1) * per, nh - 1)
    win_tokens = NA_TILE + 2 * NA_HALO
    return pl.pallas_call(
        functools.partial(_na_kernel, n_tiles=nt),
        grid=(bsz, nt),
        in_specs=[
            pl.BlockSpec((1, NA_TILE, MIX_WIDTH), lambda b, i: (b, i, 0)),
            pl.BlockSpec((1, MIX_WIDTH, NA_HALO), lambda b, i: (b, 0, prev_idx(i))),
            pl.BlockSpec((1, MIX_WIDTH, NA_TILE), lambda b, i: (b, 0, i)),
            pl.BlockSpec((1, MIX_WIDTH, NA_HALO), lambda b, i: (b, 0, next_idx(i))),
            pl.BlockSpec((1, NA_HALO, MIX_WIDTH), lambda b, i: (b, prev_idx(i), 0)),
            pl.BlockSpec((1, NA_TILE, MIX_WIDTH), lambda b, i: (b, i, 0)),
            pl.BlockSpec((1, NA_HALO, MIX_WIDTH), lambda b, i: (b, next_idx(i), 0)),
            _const_spec(bias.shape),
        ],
        out_specs=pl.BlockSpec((1, NA_TILE, MIX_WIDTH), lambda b, i: (b, i, 0)),
        out_shape=jax.ShapeDtypeStruct((bsz, seq, MIX_WIDTH), _BF16),
        scratch_shapes=[
            pltpu.VMEM((NA_HP_UNROLL, LANES, win_tokens), _BF16),
            pltpu.VMEM((NA_HP_UNROLL, LANES, win_tokens), _BF16),
            pltpu.VMEM((NA_HP_UNROLL, win_tokens, 2 * LANES), _BF16),
            pltpu.VMEM((NA_HP_UNROLL * NA_TILE_ROWS, 2 * GRID_W, NA_KEYS), _F32),
        ],
        compiler_params=pltpu.CompilerParams(
            dimension_semantics=("arbitrary", "arbitrary"), vmem_limit_bytes=VMEM_LIMIT),
        name="na_attention",
    )(q, kt, kt, kt, v, v, v, bias)


def _na_tail_kernel(x_ref, mix_ref, nw_ref, wqm_ref, wg0_ref, wg1_ref, wg2_ref, wgm_ref, wout_ref,
                    mkv_ref, fw_ref, o_ref, y_scr):
    total = x_ref.shape[1]
    for r0 in range(0, total, total // ROW_PASSES):
        rows = slice(r0, r0 + total // ROW_PASSES)
        x = x_ref[0, rows]
        h = _rms(x, nw_ref[...]).astype(_BF16)
        ys = y_scr.at[rows]
        for j, wg_ref in enumerate((wg0_ref, wg1_ref, wg2_ref)):
            c0 = j * CHUNK
            g = _dot(h, wg_ref[0])
            ys[:, c0:c0 + CHUNK] = (mix_ref[0, rows, c0:c0 + CHUNK].astype(_F32) * _silu(g)).astype(_BF16)
        _memory_attention_into(ys, h, wqm_ref[0], wgm_ref[0], mkv_ref[0, 0])
        o_ref[0, rows] = _rms(x + _dot(ys[...], wout_ref[0]), fw_ref[...])


def _na_tail(x, mix, mkv, layer, b_off, nw, win, wout, fw):
    bsz, seq, _ = x.shape
    tm = TOKEN_TILE
    assert CHUNK == XATTN_WIDTH and OFF_QMEM % CHUNK == 0
    wblock = (1, D_MODEL, CHUNK)
    first = OFF_QMEM // CHUNK
    return pl.pallas_call(
        _na_tail_kernel,
        grid=(bsz, seq // tm),
        in_specs=[
            pl.BlockSpec((1, tm, D_MODEL), lambda b, i: (b, i, 0)),
            pl.BlockSpec((1, tm, MIX_WIDTH), lambda b, i: (b, i, 0)),
            _const_spec((1, D_MODEL)),
        ] + [_fixed_spec(wblock, (layer, 0, first + n)) for n in range(5)] + [
            _fixed_spec((1,) + wout.shape[1:], (layer, 0, 0)),
            pl.BlockSpec((1, 1, N_MEM, 2 * XATTN_WIDTH), lambda b, i: (layer, b + b_off, 0, 0)),
            _const_spec((1, D_MODEL)),
        ],
        out_specs=pl.BlockSpec((1, tm, D_MODEL), lambda b, i: (b, i, 0)),
        out_shape=jax.ShapeDtypeStruct(x.shape, _F32),
        scratch_shapes=[pltpu.VMEM((tm, BRANCH_WIDTH), _BF16)],
        compiler_params=pltpu.CompilerParams(
            dimension_semantics=("arbitrary", "arbitrary"), vmem_limit_bytes=VMEM_LIMIT),
        name="na_tail",
    )(x, mix, nw, win, win, win, win, win, wout, mkv, fw)


def kernel(x_prompt, x_sample, mem_prompt, mem_sample, norm_w, w_in, w_out, mem_norm_w, w_mem_kv,
           conv_w, conv_b, na_rpb, final_norm_w):
    assert w_in.shape[0] == 2 and conv_w.shape[0] == 1 and na_rpb.shape[0] == 1
    win = w_in.astype(_BF16)
    wout = w_out.astype(_BF16)
    mem_all = jnp.concatenate([mem_prompt, mem_sample], axis=0)
    mkv = _mem_kv(mem_all, mem_norm_w, w_mem_kv.astype(_BF16))
    nw0 = norm_w[0].reshape(1, D_MODEL)
    nw1 = norm_w[1].reshape(1, D_MODEL)
    fw = final_norm_w.reshape(1, D_MODEL)
    cw = conv_w[0]
    cb = conv_b[0].reshape(1, MIX_WIDTH)
    wkt = win[1, :, OFF_P1:OFF_P2].T
    bias = _na_bias_table(na_rpb[0])

    def trunk(x, b_off):
        x1 = _conv_layer(x, mkv, 0, b_off, nw0, win, wout, cw, cb)
        q, kt, v = _qkv(x1, 1, nw1, win, wkt)
        mix = _na(q, kt, v, bias)
        return _na_tail(x1, mix, mkv, 1, b_off, nw1, win, wout, fw)

    return (trunk(x_prompt, 0), trunk(x_sample, mem_prompt.shape[0]))
```

```python
import functools

import numpy as np
import jax
import jax.numpy as jnp
from jax import lax
from jax.experimental import pallas as pl
from jax.experimental.pallas import tpu as pltpu

D_MODEL = 1024
GRID_W = 64
N_MEM = 256
MIX_WIDTH = 1536
XATTN_WIDTH = 512
BRANCH_WIDTH = MIX_WIDTH + XATTN_WIDTH
NA_HEAD_DIM = 64
NA_HEADS = MIX_WIDTH // NA_HEAD_DIM
NA_HEAD_PAIRS = NA_HEADS // 2
XATTN_HEADS = 4
XATTN_HEAD_DIM = XATTN_WIDTH // XATTN_HEADS
NA_WIN_H = 8
NA_WIN_W = 16
RMS_EPS = 1e-6
NEG_INF = -1e30

OFF_P1 = MIX_WIDTH
OFF_P2 = 2 * MIX_WIDTH
OFF_QMEM = 3 * MIX_WIDTH
OFF_GATE = 3 * MIX_WIDTH + XATTN_WIDTH

LANES = 128
F32_SUBLANES = 8
TOKEN_TILE = 1024
ROW_PASSES = 2
CONV_CHUNK = 256
CHUNK = 512
NA_TILE_ROWS = 8
NA_TILE = NA_TILE_ROWS * GRID_W
NA_HALO = 4 * GRID_W
NA_KEYS = NA_WIN_H * GRID_W
NA_HP_UNROLL = 6
VMEM_LIMIT = 56 * 1024 * 1024

_BF16 = jnp.bfloat16
_F32 = jnp.float32


def _rms(x, w):
    return x * lax.rsqrt(jnp.mean(x * x, axis=-1, keepdims=True) + RMS_EPS) * w


def _silu(g):
    return g * (1.0 / (1.0 + jnp.exp(-g)))


def _dot(a, b):
    return jnp.dot(a, b, preferred_element_type=_F32)


def _dot_nt(a, b):
    return lax.dot_general(a, b, (((1,), (1,)), ((), ())), preferred_element_type=_F32)


def _fixed_spec(block_shape, block_index):
    return pl.BlockSpec(block_shape, lambda *_: block_index, pipeline_mode=pl.Buffered(1))


def _const_spec(shape):
    return _fixed_spec(shape, (0,) * len(shape))


def _mem_kv_kernel(mem_ref, nw_ref, w_ref, o_ref):
    hm = _rms(mem_ref[...], nw_ref[0]).astype(_BF16)
    o_ref[0] = _dot(hm, w_ref[0]).astype(_BF16)


def _mem_kv(mem_all, mem_norm_w, w_mem_kv_bf16):
    depth = w_mem_kv_bf16.shape[0]
    nb = mem_all.shape[0]
    rows = nb * N_MEM
    mkv = pl.pallas_call(
        _mem_kv_kernel,
        grid=(depth,),
        in_specs=[
            pl.BlockSpec((rows, D_MODEL), lambda l: (0, 0)),
            pl.BlockSpec((1, 1, D_MODEL), lambda l: (l, 0, 0)),
            pl.BlockSpec((1, D_MODEL, 2 * XATTN_WIDTH), lambda l: (l, 0, 0)),
        ],
        out_specs=pl.BlockSpec((1, rows, 2 * XATTN_WIDTH), lambda l: (l, 0, 0)),
        out_shape=jax.ShapeDtypeStruct((depth, rows, 2 * XATTN_WIDTH), _BF16),
        compiler_params=pltpu.CompilerParams(
            dimension_semantics=("arbitrary",), vmem_limit_bytes=VMEM_LIMIT),
        name="mem_kv",
    )(mem_all.reshape(rows, D_MODEL), mem_norm_w.reshape(depth, 1, D_MODEL), w_mem_kv_bf16)
    return mkv.reshape(depth, nb, N_MEM, 2 * XATTN_WIDTH)


def _memory_attention_into(y_scr, h, wq, wg, mkv):
    qm = _dot(h, wq)
    gm = _dot(h, wg)
    scale = XATTN_HEAD_DIM ** -0.5
    for hd in range(XATTN_HEADS):
        sl = slice(hd * XATTN_HEAD_DIM, (hd + 1) * XATTN_HEAD_DIM)
        q = qm[:, sl].astype(_BF16)
        k = mkv[:, sl]
        v = mkv[:, XATTN_WIDTH + hd * XATTN_HEAD_DIM: XATTN_WIDTH + (hd + 1) * XATTN_HEAD_DIM]
        s = _dot_nt(q, k) * scale
        e = jnp.exp(s - jnp.max(s, axis=-1, keepdims=True))
        o = _dot(e.astype(_BF16), v) / jnp.sum(e, axis=-1, keepdims=True)
        y_scr[:, MIX_WIDTH + hd * XATTN_HEAD_DIM: MIX_WIDTH + (hd + 1) * XATTN_HEAD_DIM] = (
            o * _silu(gm[:, sl])).astype(_BF16)


def _conv_layer_kernel(x_ref, xp_ref, xn_ref, nw_ref, win_ref, wout_ref, mkv_ref, cw_ref, cb_ref,
                       o_ref, y_scr, *, n_tiles):
    halo = xp_ref.shape[1]
    i = pl.program_id(1)
    last = n_tiles - 1
    nw = nw_ref[...]
    win_ref = win_ref.at[0]
    wout_ref = wout_ref.at[0]
    total = x_ref.shape[1]
    tm = total // ROW_PASSES
    ext = tm + 2 * halo
    for r0 in range(0, total, tm):
        rows = slice(r0, r0 + tm)
        x = x_ref[0, rows]
        h = _rms(x, nw).astype(_BF16)
        if r0 == 0:
            prev = jnp.where(i == 0, 0.0, _rms(xp_ref[0], nw))
        else:
            prev = _rms(x_ref[0, r0 - halo:r0], nw)
        if r0 + tm == total:
            nxt = jnp.where(i == last, 0.0, _rms(xn_ref[0], nw))
        else:
            nxt = _rms(x_ref[0, r0 + tm:r0 + tm + halo], nw)
        h_ext = jnp.concatenate([h, jnp.concatenate([nxt, prev], axis=0).astype(_BF16)], axis=0)
        ys = y_scr.at[rows]
        for c0 in range(0, MIX_WIDTH, CONV_CHUNK):
            cols = slice(c0, c0 + CONV_CHUNK)
            c = _dot(h_ext, win_ref[:, OFF_P1 + c0: OFF_P1 + c0 + CONV_CHUNK])
            u = _dot(h_ext, win_ref[:, OFF_P2 + c0: OFF_P2 + c0 + CONV_CHUNK])
            v = c * u
            v_prev = pltpu.roll(v, 1, 0)[0:tm]
            v_next = pltpu.roll(v, ext - 1, 0)[0:tm]
            v_cur = v[0:tm]
            cw = cw_ref[:, cols]
            conv = v_prev * cw[0:1] + v_cur * cw[1:2] + v_next * cw[2:3] + cb_ref[:, cols]
            bg = _dot(h, win_ref[:, cols])
            g = _dot(h, win_ref[:, OFF_GATE + c0: OFF_GATE + c0 + CONV_CHUNK])
            ys[:, cols] = (bg * conv * _silu(g)).astype(_BF16)
        _memory_attention_into(ys, h, win_ref[:, OFF_QMEM:OFF_QMEM + XATTN_WIDTH],
                               win_ref[:, OFF_GATE + MIX_WIDTH:], mkv_ref[0, 0])
        o_ref[0, rows] = x + _dot(ys[...], wout_ref[...])


def _conv_layer(x, mkv, layer, b_off, nw, win, wout, cw, cb):
    bsz, seq, _ = x.shape
    tm = TOKEN_TILE
    halo = F32_SUBLANES
    nt = seq // tm
    per = tm // halo
    return pl.pallas_call(
        functools.partial(_conv_layer_kernel, n_tiles=nt),
        grid=(bsz, nt),
        in_specs=[
            pl.BlockSpec((1, tm, D_MODEL), lambda b, i: (b, i, 0)),
            pl.BlockSpec((1, halo, D_MODEL), lambda b, i: (b, jnp.maximum(i * per - 1, 0), 0)),
            pl.BlockSpec((1, halo, D_MODEL), lambda b, i: (b, jnp.minimum((i + 1) * per, nt * per - 1), 0)),
            _const_spec((1, D_MODEL)),
            _fixed_spec((1,) + win.shape[1:], (layer, 0, 0)),
            _fixed_spec((1,) + wout.shape[1:], (layer, 0, 0)),
            pl.BlockSpec((1, 1, N_MEM, 2 * XATTN_WIDTH), lambda b, i: (layer, b + b_off, 0, 0)),
            _const_spec(cw.shape),
            _const_spec(cb.shape),
        ],
        out_specs=pl.BlockSpec((1, tm, D_MODEL), lambda b, i: (b, i, 0)),
        out_shape=jax.ShapeDtypeStruct(x.shape, _F32),
        scratch_shapes=[pltpu.VMEM((tm, BRANCH_WIDTH), _BF16)],
        compiler_params=pltpu.CompilerParams(
            dimension_semantics=("arbitrary", "arbitrary"), vmem_limit_bytes=VMEM_LIMIT),
        name="conv_layer",
    )(x, x, x, nw, win, wout, mkv, cw, cb)


def _qkv_kernel(x_ref, nw_ref, wq_ref, wkt_ref, wv_ref, q_ref, kt_ref, v_ref):
    h = _rms(x_ref[0], nw_ref[...]).astype(_BF16)
    q_ref[0] = (_dot(h, wq_ref[0]) * (NA_HEAD_DIM ** -0.5)).astype(_BF16)
    kt_ref[0] = _dot_nt(wkt_ref[...], h).astype(_BF16)
    v_ref[0] = _dot(h, wv_ref[0]).astype(_BF16)


def _qkv(x, layer, nw, win, wkt):
    bsz, seq, _ = x.shape
    tm = TOKEN_TILE
    wblock = (1, D_MODEL, MIX_WIDTH)
    return pl.pallas_call(
        _qkv_kernel,
        grid=(bsz, seq // tm),
        in_specs=[
            pl.BlockSpec((1, tm, D_MODEL), lambda b, i: (b, i, 0)),
            _const_spec((1, D_MODEL)),
            _fixed_spec(wblock, (layer, 0, 0)),
            _const_spec(wkt.shape),
            _fixed_spec(wblock, (layer, 0, OFF_P2 // MIX_WIDTH)),
        ],
        out_specs=[
            pl.BlockSpec((1, tm, MIX_WIDTH), lambda b, i: (b, i, 0)),
            pl.BlockSpec((1, MIX_WIDTH, tm), lambda b, i: (b, 0, i)),
            pl.BlockSpec((1, tm, MIX_WIDTH), lambda b, i: (b, i, 0)),
        ],
        out_shape=[
            jax.ShapeDtypeStruct((bsz, seq, MIX_WIDTH), _BF16),
            jax.ShapeDtypeStruct((bsz, MIX_WIDTH, seq), _BF16),
            jax.ShapeDtypeStruct((bsz, seq, MIX_WIDTH), _BF16),
        ],
        compiler_params=pltpu.CompilerParams(
            dimension_semantics=("arbitrary", "arbitrary"), vmem_limit_bytes=VMEM_LIMIT),
        name="qkv_proj",
    )(x, nw, win, wkt, win)


def _na_bias_table(rpb):
    c = np.arange(GRID_W)[:, None]
    kc = np.arange(GRID_W)[None, :]
    cstart = np.clip(c - NA_WIN_W // 2, 0, GRID_W - NA_WIN_W)
    valid = (kc >= cstart) & (kc < cstart + NA_WIN_W)
    dx = kc - c + NA_WIN_W - 1
    select = ((np.arange(2 * NA_WIN_W - 1)[:, None, None] == dx[None]) & valid[None]).astype(np.float32)
    mask = np.full((GRID_W, 2 * NA_WIN_H, GRID_W), NEG_INF, np.float32)
    mask[:, 1:, :] = np.where(valid, 0.0, NEG_INF)[:, None, :]
    rpb_padded = jnp.pad(rpb, ((0, 0), (1, 0), (0, 0)))
    tab = jnp.einsum("hyd,dck->hcyk", rpb_padded, jnp.asarray(select),
                     precision=lax.Precision.HIGHEST) + jnp.asarray(mask)[None]
    return tab.reshape(NA_HEAD_PAIRS, 2 * GRID_W, 2 * NA_WIN_H * GRID_W)


def _na_kernel(q_ref, kp_ref, km_ref, kn_ref, vp_ref, vm_ref, vn_ref, bias_ref, o_ref,
               kbuf, kshift, vbuf, s_scr, *, n_tiles):
    i = pl.program_id(1)
    last = n_tiles - 1
    win_tokens = NA_TILE + 2 * NA_HALO

    @pl.when(jnp.logical_and(pl.program_id(0) == 0, i == 0))
    def _():
        vbuf[...] = jnp.ones(vbuf.shape, vbuf.dtype)

    lane = lax.broadcasted_iota(jnp.int32, (GRID_W, LANES), 1)
    low_half = lane < NA_HEAD_DIM

    def tile_variant(lo, hi):
        def window(qi):
            r0 = min(max(qi - NA_WIN_H // 2, lo), hi)
            return NA_HALO + r0 * GRID_W, NA_WIN_H - 1 - qi + r0

        def load_windows(hp, u):
            c0 = pl.multiple_of(hp * LANES, LANES)
            kbuf[u, :, 0:NA_HALO] = kp_ref[0, pl.ds(c0, LANES), :]
            kbuf[u, :, NA_HALO:NA_HALO + NA_TILE] = km_ref[0, pl.ds(c0, LANES), :]
            kbuf[u, :, NA_HALO + NA_TILE:] = kn_ref[0, pl.ds(c0, LANES), :]
            vbuf[u, 0:NA_HALO, 0:LANES] = vp_ref[0, :, pl.ds(c0, LANES)]
            vbuf[u, NA_HALO:NA_HALO + NA_TILE, 0:LANES] = vm_ref[0, :, pl.ds(c0, LANES)]
            vbuf[u, NA_HALO + NA_TILE:, 0:LANES] = vn_ref[0, :, pl.ds(c0, LANES)]
            kw32 = pltpu.bitcast(kbuf[u], jnp.uint32)
            kshift[u] = pltpu.bitcast(pltpu.roll(kw32, win_tokens - GRID_W, 1), _BF16)

        def scores(hp, u, qi):
            c0 = pl.multiple_of(hp * LANES, LANES)
            tok, dy0 = window(qi)
            rows = slice(qi * GRID_W, (qi + 1) * GRID_W)
            q2 = q_ref[0, rows, pl.ds(c0, LANES)]
            zero = jnp.zeros_like(q2)
            qs = jnp.concatenate([jnp.where(low_half, q2, zero),
                                  jnp.where(low_half, zero, q2)], axis=0)
            if tok % LANES == 0:
                kw = kbuf[u, :, tok:tok + NA_KEYS]
            else:
                kw = kshift[u, :, tok - GRID_W:tok - GRID_W + NA_KEYS]
            b0 = (dy0 + 1) * GRID_W
            if b0 % LANES == 0:
                bias = bias_ref[hp, :, b0:b0 + NA_KEYS]
            else:
                wide = bias_ref[hp, :, b0 - GRID_W:b0 - GRID_W + NA_KEYS + LANES]
                bias = pltpu.roll(wide, NA_KEYS + LANES - GRID_W, 1)[:, 0:NA_KEYS]
            s = _dot(qs, kw) + bias
            s_scr[u * NA_TILE_ROWS + qi] = s
            return jnp.max(s, axis=-1, keepdims=True)

        def finish(hp, u, qi, row_max):
            c0 = pl.multiple_of(hp * LANES, LANES)
            tok, _ = window(qi)
            rows = slice(qi * GRID_W, (qi + 1) * GRID_W)
            e = jnp.exp((s_scr[u * NA_TILE_ROWS + qi] - row_max).astype(_BF16))
            vw = vbuf[u, tok:tok + NA_KEYS, :]
            od = _dot(e, vw)
            o = od[:, 0:LANES] / od[:, LANES:]
            o_ref[0, rows, pl.ds(c0, LANES)] = jnp.where(
                low_half, o[0:GRID_W], o[GRID_W:]).astype(o_ref.dtype)

        def hp_body(j, carry):
            for u in range(NA_HP_UNROLL):
                hp = j * NA_HP_UNROLL + u
                load_windows(hp, u)
                maxes = [scores(hp, u, qi) for qi in range(NA_TILE_ROWS)]
                for qi in range(NA_TILE_ROWS):
                    finish(hp, u, qi, maxes[qi])
            return carry

        lax.fori_loop(0, NA_HEAD_PAIRS // NA_HP_UNROLL, hp_body, 0)

    big = NA_TILE_ROWS

    @pl.when(i == 0)
    def _():
        tile_variant(0, big)

    @pl.when(i == last)
    def _():
        tile_variant(-big, 0)

    @pl.when(jnp.logical_and(i > 0, i < last))
    def _():
        tile_variant(-big, big)


def _na(q, kt, v, bias):
    bsz, seq, _ = q.shape
    nt = seq // NA_TILE
    assert nt >= 2 and seq % NA_TILE == 0
    per = NA_TILE // NA_HALO
    nh = seq // NA_HALO
    prev_idx = lambda i: jnp.maximum(i * per - 1, 0)
    next_idx = lambda i: jnp.minimum((i + 1) * per, nh - 1)
    win_tokens = NA_TILE + 2 * NA_HALO
    return pl.pallas_call(
        functools.partial(_na_kernel, n_tiles=nt),
        grid=(bsz, nt),
        in_specs=[
            pl.BlockSpec((1, NA_TILE, MIX_WIDTH), lambda b, i: (b, i, 0)),
            pl.BlockSpec((1, MIX_WIDTH, NA_HALO), lambda b, i: (b, 0, prev_idx(i))),
            pl.BlockSpec((1, MIX_WIDTH, NA_TILE), lambda b, i: (b, 0, i)),
            pl.BlockSpec((1, MIX_WIDTH, NA_HALO), lambda b, i: (b, 0, next_idx(i))),
            pl.BlockSpec((1, NA_HALO, MIX_WIDTH), lambda b, i: (b, prev_idx(i), 0)),
            pl.BlockSpec((1, NA_TILE, MIX_WIDTH), lambda b, i: (b, i, 0)),
            pl.BlockSpec((1, NA_HALO, MIX_WIDTH), lambda b, i: (b, next_idx(i), 0)),
            _const_spec(bias.shape),
        ],
        out_specs=pl.BlockSpec((1, NA_TILE, MIX_WIDTH), lambda b, i: (b, i, 0)),
        out_shape=jax.ShapeDtypeStruct((bsz, seq, MIX_WIDTH), _BF16),
        scratch_shapes=[
            pltpu.VMEM((NA_HP_UNROLL, LANES, win_tokens), _BF16),
            pltpu.VMEM((NA_HP_UNROLL, LANES, win_tokens), _BF16),
            pltpu.VMEM((NA_HP_UNROLL, win_tokens, 2 * LANES), _BF16),
            pltpu.VMEM((NA_HP_UNROLL * NA_TILE_ROWS, 2 * GRID_W, NA_KEYS), _F32),
        ],
        compiler_params=pltpu.CompilerParams(
            dimension_semantics=("arbitrary", "arbitrary"), vmem_limit_bytes=VMEM_LIMIT),
        name="na_attention",
    )(q, kt, kt, kt, v, v, v, bias)


def _na_tail_kernel(x_ref, mix_ref, nw_ref, wqm_ref, wg0_ref, wg1_ref, wg2_ref, wgm_ref, wout_ref,
                    mkv_ref, fw_ref, o_ref, y_scr):
    total = x_ref.shape[1]
    for r0 in range(0, total, total // ROW_PASSES):
        rows = slice(r0, r0 + total // ROW_PASSES)
        x = x_ref[0, rows]
        h = _rms(x, nw_ref[...]).astype(_BF16)
        ys = y_scr.at[rows]
        for j, wg_ref in enumerate((wg0_ref, wg1_ref, wg2_ref)):
            c0 = j * CHUNK
            g = _dot(h, wg_ref[0])
            ys[:, c0:c0 + CHUNK] = (mix_ref[0, rows, c0:c0 + CHUNK].astype(_F32) * _silu(g)).astype(_BF16)
        _memory_attention_into(ys, h, wqm_ref[0], wgm_ref[0], mkv_ref[0, 0])
        o_ref[0, rows] = _rms(x + _dot(ys[...], wout_ref[0]), fw_ref[...])


def _na_tail(x, mix, mkv, layer, b_off, nw, win, wout, fw):
    bsz, seq, _ = x.shape
    tm = TOKEN_TILE
    assert CHUNK == XATTN_WIDTH and OFF_QMEM % CHUNK == 0
    wblock = (1, D_MODEL, CHUNK)
    first = OFF_QMEM // CHUNK
    return pl.pallas_call(
        _na_tail_kernel,
        grid=(bsz, seq // tm),
        in_specs=[
            pl.BlockSpec((1, tm, D_MODEL), lambda b, i: (b, i, 0)),
            pl.BlockSpec((1, tm, MIX_WIDTH), lambda b, i: (b, i, 0)),
            _const_spec((1, D_MODEL)),
        ] + [_fixed_spec(wblock, (layer, 0, first + n)) for n in range(5)] + [
            _fixed_spec((1,) + wout.shape[1:], (layer, 0, 0)),
            pl.BlockSpec((1, 1, N_MEM, 2 * XATTN_WIDTH), lambda b, i: (layer, b + b_off, 0, 0)),
            _const_spec((1, D_MODEL)),
        ],
        out_specs=pl.BlockSpec((1, tm, D_MODEL), lambda b, i: (b, i, 0)),
        out_shape=jax.ShapeDtypeStruct(x.shape, _F32),
        scratch_shapes=[pltpu.VMEM((tm, BRANCH_WIDTH), _BF16)],
        compiler_params=pltpu.CompilerParams(
            dimension_semantics=("arbitrary", "arbitrary"), vmem_limit_bytes=VMEM_LIMIT),
        name="na_tail",
    )(x, mix, nw, win, win, win, win, win, wout, mkv, fw)


def kernel(x_prompt, x_sample, mem_prompt, mem_sample, norm_w, w_in, w_out, mem_norm_w, w_mem_kv,
           conv_w, conv_b, na_rpb, final_norm_w):
    assert w_in.shape[0] == 2 and conv_w.shape[0] == 1 and na_rpb.shape[0] == 1
    win = w_in.astype(_BF16)
    wout = w_out.astype(_BF16)
    mem_all = jnp.concatenate([mem_prompt, mem_sample], axis=0)
    mkv = _mem_kv(mem_all, mem_norm_w, w_mem_kv.astype(_BF16))
    nw0 = norm_w[0].reshape(1, D_MODEL)
    nw1 = norm_w[1].reshape(1, D_MODEL)
    fw = final_norm_w.reshape(1, D_MODEL)
    cw = conv_w[0]
    cb = conv_b[0].reshape(1, MIX_WIDTH)
    wkt = win[1, :, OFF_P1:OFF_P2].T
    bias = _na_bias_table(na_rpb[0])

    def trunk(x, b_off):
        x1 = _conv_layer(x, mkv, 0, b_off, nw0, win, wout, cw, cb)
        q, kt, v = _qkv(x1, 1, nw1, win, wkt)
        mix = _na(q, kt, v, bias)
        return _na_tail(x1, mix, mkv, 1, b_off, nw1, win, wout, fw)

    return (trunk(x_prompt, 0), trunk(x_sample, mem_prompt.shape[0]))
```
